```python
import math
import jax, jax.numpy as jnp
from jax import lax
import numpy as np

D_MODEL = 4096
BATCH = 4
SEQ = 4096
DEPTH = 2

CHUNK = 64
Q_BLOCK = 128
HEAD_DIM = 128
FOX_HEADS = 16
DSA_HEADS = 16
FOX_WIDTH = FOX_HEADS * HEAD_DIM
DSA_WIDTH = DSA_HEADS * HEAD_DIM
IDX_HEADS = 32
IDX_DIM = 64
TOPK_MAX = 256
D_FF = ((8 * D_MODEL + 3 * 256 - 1) // (3 * 256)) * 256
ROPE_THETA = 10000.0
EPS = 1e-6
N_MOD = 6
IN_SIZES = (FOX_WIDTH, FOX_WIDTH, FOX_WIDTH, FOX_HEADS,
            DSA_WIDTH, HEAD_DIM, HEAD_DIM,
            IDX_HEADS * IDX_DIM, IDX_DIM, IDX_HEADS)
D_IN = 3 * FOX_WIDTH + FOX_HEADS + DSA_WIDTH + 2 * HEAD_DIM + IDX_HEADS * IDX_DIM + IDX_DIM + IDX_HEADS

kernel_name = "hybrid_fox_dsa_gated_stream_encoder"


def rms_norm(x, g):
    xf = x.astype(jnp.float32)
    y = xf * lax.rsqrt(jnp.mean(xf * xf, axis=-1, keepdims=True) + EPS)
    return (y * g.astype(jnp.float32)).astype(x.dtype)


def rope_tables(positions, dim):
    inv = ROPE_THETA ** (-jnp.arange(0, dim, 2, dtype=jnp.float32) / dim)
    ang = positions.astype(jnp.float32)[..., None] * inv
    return jnp.cos(ang)[:, :, None, :], jnp.sin(ang)[:, :, None, :]


def apply_rope(x, cos, sin):
    xf = x.astype(jnp.float32)
    x1, x2 = jnp.split(xf, 2, axis=-1)
    return jnp.concatenate([x1 * cos - x2 * sin, x2 * cos + x1 * sin], axis=-1).astype(x.dtype)


def split_in(z):
    offsets = []
    acc = 0
    for n in IN_SIZES[:-1]:
        acc += n
        offsets.append(acc)
    return jnp.split(z, offsets, axis=-1)


def fox_attention(q, k, v, log_f):
    B, S, H, Dh = q.shape
    nblk = S // Q_BLOCK
    cum = jnp.cumsum(log_f, axis=1).transpose(0, 2, 1)
    scale = Dh ** -0.5
    kpos = jnp.arange(S)
    qb = q.reshape(B, nblk, Q_BLOCK, H, Dh).transpose(1, 0, 2, 3, 4)
    cb = cum.reshape(B, H, nblk, Q_BLOCK).transpose(2, 0, 1, 3)

    def block(args):
        i, q_i, c_i = args
        qpos = i * Q_BLOCK + jnp.arange(Q_BLOCK)
        s = jnp.einsum('bthd,bshd->bhts', q_i, k).astype(jnp.float32) * scale
        s = s + c_i[..., :, None] - cum[:, :, None, :]
        mask = kpos[None, :] <= qpos[:, None]
        s = jnp.where(mask, s, -jnp.inf)
        p = jax.nn.softmax(s, axis=-1).astype(v.dtype)
        return jnp.einsum('bhts,bshd->bthd', p, v)

    out = lax.map(block, (jnp.arange(nblk), qb, cb))
    return out.transpose(1, 0, 2, 3, 4).reshape(B, S, H * Dh)


def dsa_attention(q, k, v, q_idx, k_idx, w_idx, top_k):
    B, S, H, Dh = q.shape
    nblk = S // Q_BLOCK
    scale = Dh ** -0.5
    kchunk = jnp.arange(S) // CHUNK
    qb = q.reshape(B, nblk, Q_BLOCK, H, Dh).transpose(1, 0, 2, 3, 4)
    qib = q_idx.reshape(B, nblk, Q_BLOCK, IDX_HEADS, IDX_DIM).transpose(1, 0, 2, 3, 4)
    wb = w_idx.reshape(B, nblk, Q_BLOCK, IDX_HEADS).transpose(1, 0, 2, 3)
    gather = jax.vmap(lambda tb, ib: tb[ib])

    def block(args):
        i, q_i, qi_i, w_i = args
        qchunk = (i * Q_BLOCK + jnp.arange(Q_BLOCK)) // CHUNK
        adm = kchunk[None, :] <= qchunk[:, None]
        rel = jax.nn.relu(jnp.einsum('bthe,bse->bths', qi_i, k_idx))
        score = jnp.einsum('bths,bth->bts', rel, w_i).astype(jnp.float32)
        score = jnp.where(adm[None], score, -jnp.inf)
        _, idx = lax.top_k(score, top_k)
        valid = kchunk[idx] <= qchunk[None, :, None]
        kg = gather(k, idx)
        vg = gather(v, idx)
        s = jnp.einsum('bthd,btkd->bhtk', q_i, kg).astype(jnp.float32) * scale
        s = jnp.where(valid[:, None], s, -jnp.inf)
        p = jax.nn.softmax(s, axis=-1).astype(vg.dtype)
        return jnp.einsum('bhtk,btkd->bthd', p, vg)

    out = lax.map(block, (jnp.arange(nblk), qb, qib, wb))
    return out.transpose(1, 0, 2, 3, 4).reshape(B, S, H * Dh)


def setup_inputs(seed: int = 0) -> dict:
    key = jax.random.key(seed)
    ks = jax.random.split(key, 16)
    f32 = jnp.float32
    nrm = lambda k, shape, fan_in: jax.random.normal(k, shape, f32) * (fan_in ** -0.5)
    x = jax.random.normal(ks[0], (BATCH, SEQ, D_MODEL), f32)
    c = jax.random.normal(ks[1], (BATCH, D_MODEL), f32)
    offset = jax.random.randint(ks[2], (BATCH, 1), 0, 64, dtype=jnp.int32) * CHUNK
    positions = (offset + jnp.arange(SEQ, dtype=jnp.int32)[None, :]).astype(jnp.int32)
    w_ada = nrm(ks[3], (D_MODEL, N_MOD * D_MODEL), D_MODEL) * 0.5
    b_ada = 0.02 * jax.random.normal(ks[4], (DEPTH, N_MOD, D_MODEL), f32)
    norm_g = 1.0 + 0.02 * jax.random.normal(ks[5], (DEPTH, 4, D_MODEL), f32)
    w_in = nrm(ks[6], (DEPTH, D_MODEL, D_IN), D_MODEL)
    b_forget = 3.0 + 0.1 * jax.random.normal(ks[7], (DEPTH, FOX_HEADS), f32)
    w_proj_fox = nrm(ks[8], (DEPTH, FOX_WIDTH, D_MODEL), FOX_WIDTH)
    w_proj_dsa = nrm(ks[9], (DEPTH, DSA_WIDTH, D_MODEL), DSA_WIDTH)
    w_gate = nrm(ks[10], (DEPTH, D_MODEL, 2 * D_MODEL), D_MODEL)
    w_out = nrm(ks[11], (DEPTH, D_MODEL, D_MODEL), D_MODEL)
    w_ff_gate = nrm(ks[12], (DEPTH, D_MODEL, D_FF), D_MODEL)
    w_ff_up = nrm(ks[13], (DEPTH, D_MODEL, D_FF), D_MODEL)
    w_ff_down = nrm(ks[14], (DEPTH, D_FF, D_MODEL), D_FF)
    return {"x": x, "c": c, "positions": positions, "w_ada": w_ada, "b_ada": b_ada,
            "norm_g": norm_g, "w_in": w_in, "b_forget": b_forget,
            "w_proj_fox": w_proj_fox, "w_proj_dsa": w_proj_dsa, "w_gate": w_gate,
            "w_out": w_out, "w_ff_gate": w_ff_gate, "w_ff_up": w_ff_up,
            "w_ff_down": w_ff_down}


def reference(x, c, positions, w_ada, b_ada, norm_g, w_in, b_forget, w_proj_fox,
              w_proj_dsa, w_gate, w_out, w_ff_gate, w_ff_up, w_ff_down):
    B, S, D = x.shape
    top_k = min(TOPK_MAX, S // 4)
    cos, sin = rope_tables(positions, HEAD_DIM)
    cos_i, sin_i = rope_tables(positions, IDX_DIM)
    idx_w_scale = (IDX_HEADS * IDX_DIM) ** -0.5
    ada = (jax.nn.silu(c) @ w_ada).reshape(B, N_MOD, D)
    for l in range(DEPTH):
        mod = ada + b_ada[l][None]
        shift_m, scale_m, gate_m = mod[:, 0, None, :], mod[:, 1, None, :], mod[:, 2, None, :]
        shift_f, scale_f, gate_f = mod[:, 3, None, :], mod[:, 4, None, :], mod[:, 5, None, :]

        h = rms_norm(x, norm_g[l, 0]) * (1.0 + scale_m) + shift_m
        z = h @ w_in[l]
        q_a, k_a, v_a, f_a, q_b, k_b, v_b, q_i, k_i, w_i = split_in(z)
        log_f = jax.nn.log_sigmoid(f_a.astype(jnp.float32) + b_forget[l].astype(jnp.float32))
        y_a = fox_attention(q_a.reshape(B, S, FOX_HEADS, HEAD_DIM),
                            k_a.reshape(B, S, FOX_HEADS, HEAD_DIM),
                            v_a.reshape(B, S, FOX_HEADS, HEAD_DIM), log_f)
        qb = apply_rope(q_b.reshape(B, S, DSA_HEADS, HEAD_DIM), cos, sin)
        kb = apply_rope(k_b.reshape(B, S, 1, HEAD_DIM), cos, sin)[:, :, 0]
        qi = apply_rope(q_i.reshape(B, S, IDX_HEADS, IDX_DIM), cos_i, sin_i)
        ki = apply_rope(k_i.reshape(B, S, 1, IDX_DIM), cos_i, sin_i)[:, :, 0]
        y_b = dsa_attention(qb, kb, v_b, qi, ki, w_i * idx_w_scale, top_k)
        g_a, g_b = jnp.split(jax.nn.sigmoid(h @ w_gate[l]), 2, axis=-1)
        mixed = (g_a * (y_a @ w_proj_fox[l]) + g_b * (y_b @ w_proj_dsa[l])) @ w_out[l]
        x = x + gate_m * rms_norm(mixed, norm_g[l, 1])

        h = rms_norm(x, norm_g[l, 2]) * (1.0 + scale_f) + shift_f
        ff = (jax.nn.silu(h @ w_ff_gate[l]) * (h @ w_ff_up[l])) @ w_ff_down[l]
        x = x + gate_f * rms_norm(ff, norm_g[l, 3])
    return x
```

```python
import functools

import jax
import jax.numpy as jnp
from jax import lax
from jax.experimental import pallas as pl
from jax.experimental.pallas import tpu as pltpu

F32 = jnp.float32
BF16 = jnp.bfloat16

CHUNK = 64
HEAD_DIM = 128
FOX_HEADS = 16
DSA_HEADS = 16
FOX_WIDTH = FOX_HEADS * HEAD_DIM
DSA_WIDTH = DSA_HEADS * HEAD_DIM
IDX_HEADS = 32
IDX_DIM = 64
IDX_WIDTH = IDX_HEADS * IDX_DIM
TOPK_MAX = 256
ROPE_THETA = 10000.0
EPS = 1e-6
N_MOD = 6

OFF_QA = 0
OFF_FA = 3 * FOX_WIDTH
OFF_QB = OFF_FA + FOX_HEADS
OFF_KB = OFF_QB + DSA_WIDTH
OFF_QI = OFF_KB + 2 * HEAD_DIM
OFF_KI = OFF_QI + IDX_WIDTH
OFF_WI = OFF_KI + IDX_DIM
D_IN = OFF_WI + IDX_HEADS

BIG_QB = 3 * FOX_WIDTH
BIG_QI = BIG_QB + DSA_WIDTH
BIG_N = BIG_QI + IDX_WIDTH
MISC_KI = 0
MISC_WI = IDX_DIM
MISC_FA = IDX_DIM + IDX_HEADS
SMALL_N = 3 * HEAD_DIM

LANES = 128
V7X_VMEM_BYTES = 64 * 1024 * 1024
VMEM_BUDGET = V7X_VMEM_BYTES - 8 * 1024 * 1024

NEG = -1e30
INT_MIN = -(2 ** 31)


def _params(semantics, vmem_bytes):
    return pltpu.CompilerParams(dimension_semantics=semantics,
                                vmem_limit_bytes=int(min(vmem_bytes, VMEM_BUDGET)))


def _tile(n, pref):
    if n <= pref:
        return n
    t = (pref // LANES) * LANES
    while t > LANES and n % t:
        t -= LANES
    assert n % t == 0, (n, pref)
    return t


def _ada_kernel(c_ref, w_ref, b_ref, o_ref):
    c = c_ref[...]
    s = c * jax.nn.sigmoid(c)
    acc = jnp.dot(s.astype(BF16), w_ref[...].astype(BF16), preferred_element_type=F32)
    o_ref[...] = acc[None] + b_ref[...]


def _ada(c, w_ada, b_ada):
    B, D = c.shape
    L = b_ada.shape[0]
    N = w_ada.shape[1]
    rows = 8
    cp = jnp.zeros((rows, D), F32).at[:B].set(c)
    tn = _tile(N, 512)
    out = pl.pallas_call(
        _ada_kernel,
        grid=(N // tn,),
        in_specs=[pl.BlockSpec((rows, D), lambda j: (0, 0)),
                  pl.BlockSpec((D, tn), lambda j: (0, j)),
                  pl.BlockSpec((L, 1, tn), lambda j: (0, 0, j))],
        out_specs=pl.BlockSpec((L, rows, tn), lambda j: (0, 0, j)),
        out_shape=jax.ShapeDtypeStruct((L, rows, N), F32),
        compiler_params=_params(("parallel",), 4 * D * tn * 4 + (8 << 20)),
        name="ada",
    )(cp, w_ada, b_ada.reshape(L, 1, N))
    return out[:, :B].reshape(L, B, N_MOD, D)


def _rms(v, g):
    return v * lax.rsqrt(jnp.mean(v * v, axis=-1, keepdims=True) + EPS) * g


def _norm_kernel(*refs, resid, emit_h, gate_row, ga_row, scale_row, shift_row, gb_row):
    refs = list(refs)
    x_ref = refs.pop(0)
    y_ref = refs.pop(0) if resid else None
    moda_ref, modb_ref, ga_ref, gb_ref = refs[:4]
    outs = refs[4:]
    x = x_ref[0]
    if resid:
        gate = moda_ref[0, gate_row:gate_row + 1, :]
        x = x + gate * _rms(y_ref[0], ga_ref[ga_row:ga_row + 1, :])
        outs.pop(0)[0] = x
    if emit_h:
        scale = modb_ref[0, scale_row:scale_row + 1, :]
        shift = modb_ref[0, shift_row:shift_row + 1, :]
        h = _rms(x, gb_ref[gb_row:gb_row + 1, :]) * (1.0 + scale) + shift
        outs.pop(0)[0] = h.astype(BF16)


def _norm(x, y, moda, modb, ga, gb, *, gate_row=0, ga_row=0, scale_row=0, shift_row=0,
          gb_row=0, emit_h=True):
    B, S, D = x.shape
    resid = y is not None
    ts = _tile(S, 256)
    row = pl.BlockSpec((1, ts, D), lambda b, i: (b, i, 0))
    mod = pl.BlockSpec((1, N_MOD, D), lambda b, i: (b, 0, 0))
    gsp = pl.BlockSpec((4, D), lambda b, i: (0, 0))
    in_specs = [row] + ([row] if resid else []) + [mod, mod, gsp, gsp]
    args = [x] + ([y] if resid else []) + [moda, modb, ga, gb]
    out_shape, out_specs = [], []
    if resid:
        out_shape.append(jax.ShapeDtypeStruct((B, S, D), F32))
        out_specs.append(row)
    if emit_h:
        out_shape.append(jax.ShapeDtypeStruct((B, S, D), BF16))
        out_specs.append(row)
    n_rows_f32 = 1 + 2 * resid + 0.5 * emit_h
    outs = pl.pallas_call(
        functools.partial(_norm_kernel, resid=resid, emit_h=emit_h, gate_row=gate_row,
                          ga_row=ga_row, scale_row=scale_row, shift_row=shift_row,
                          gb_row=gb_row),
        grid=(B, S // ts),
        in_specs=in_specs, out_specs=out_specs, out_shape=out_shape,
        compiler_params=_params(("parallel", "parallel"),
                                2 * n_rows_f32 * ts * D * 4 + 6 * ts * D * 4 + (4 << 20)),
        name="norm",
    )(*args)
    outs = list(outs)
    x_new = outs.pop(0) if resid else None
    h = outs.pop(0) if emit_h else None
    return x_new, h


def _mm_kernel(*refs, n_a, pairs, n_extra, epilogue):
    n_w = len(pairs)
    a_refs = refs[:n_a]
    w_refs = refs[n_a:n_a + n_w]
    extra = refs[n_a + n_w:n_a + n_w + n_extra]
    outs = refs[n_a + n_w + n_extra:]
    accs = [jnp.dot(a_refs[ai][...], w_refs[wi][...], preferred_element_type=F32)
            for wi, ai in enumerate(pairs)]
    epilogue(accs, extra, outs)


def _matmul(a_list, w_list, extras, outs, epilogue, *, tm, tn, n_tiles, name):
    M = a_list[0].shape[0]
    in_specs = [pl.BlockSpec((tm, a.shape[1]), lambda i, j: (i, 0)) for a in a_list]
    vmem = sum(2 * tm * a.shape[1] * a.dtype.itemsize for a in a_list)
    for _, w, off in w_list:
        in_specs.append(pl.BlockSpec((w.shape[0], tn), lambda i, j, off=off: (0, j + off)))
        vmem += 2 * w.shape[0] * tn * w.dtype.itemsize + 2 * tm * tn * 4
    in_specs += [sp for _, sp in extras]
    vmem += sum(2 * tm * LANES * 4 for _ in extras)
    vmem += sum(2 * tm * sp.block_shape[-1] * sd.dtype.itemsize for sd, sp in outs)
    return pl.pallas_call(
        functools.partial(_mm_kernel, n_a=len(a_list), pairs=tuple(ai for ai, _, _ in w_list),
                          n_extra=len(extras), epilogue=epilogue),
        grid=(M // tm, n_tiles),
        in_specs=in_specs,
        out_specs=[sp for _, sp in outs],
        out_shape=[sd for sd, _ in outs],
        compiler_params=_params(("parallel", "arbitrary"), vmem + (4 << 20)),
        name=name,
    )(*a_list, *[w for _, w, _ in w_list], *[e for e, _ in extras])


def _rope_slab(a, cosf, sinf, half):
    if 2 * half == LANES:
        partner = pltpu.roll(a, half, 1)
    else:
        lane = lax.broadcasted_iota(jnp.int32, a.shape, 1)
        partner = jnp.where(lane % (2 * half) < half,
                            pltpu.roll(a, LANES - half, 1), pltpu.roll(a, half, 1))
    return a * cosf + partner * sinf


def _epi_in_big(accs, extra, outs, *, tn):
    cos_h, sin_h, cos_i, sin_i = extra
    (o_ref,) = outs
    acc = accs[0]
    j = pl.program_id(1)
    t_qb, t_qi = BIG_QB // tn, BIG_QI // tn

    @pl.when(j < t_qb)
    def _():
        o_ref[...] = acc.astype(BF16)

    def roped(cos_ref, sin_ref, half):
        cosf, sinf = cos_ref[...], sin_ref[...]
        for s in range(tn // LANES):
            sl = slice(s * LANES, (s + 1) * LANES)
            o_ref[:, sl] = _rope_slab(acc[:, sl], cosf, sinf, half).astype(BF16)

    @pl.when((j >= t_qb) & (j < t_qi))
    def _():
        roped(cos_h, sin_h, HEAD_DIM // 2)

    @pl.when(j >= t_qi)
    def _():
        roped(cos_i, sin_i, IDX_DIM // 2)


def _epi_in_small(accs, extra, outs, *, idx_w_scale):
    cos_h, sin_h, cos_i, sin_i, fbias = extra
    kv_ref, misc_ref = outs
    acc = accs[0]
    kb = _rope_slab(acc[:, 0:LANES], cos_h[...], sin_h[...], HEAD_DIM // 2)
    kv_ref[:, 0:LANES] = kb.astype(BF16)
    kv_ref[:, LANES:2 * LANES] = acc[:, LANES:2 * LANES].astype(BF16)
    a = acc[:, 2 * LANES:3 * LANES]
    ki = _rope_slab(a, cos_i[...], sin_i[...], IDX_DIM // 2)
    f = a + fbias[...]
    log_f = jnp.minimum(f, 0.0) - jnp.log1p(jnp.exp(-jnp.abs(f)))
    lane = lax.broadcasted_iota(jnp.int32, a.shape, 1)
    misc_ref[...] = jnp.where(lane < MISC_WI, ki,
                              jnp.where(lane < MISC_FA, a * idx_w_scale, log_f))


def _epi_merge(accs, extra, outs):
    ga, gb, pa, pb = accs
    outs[0][...] = (jax.nn.sigmoid(ga) * pa + jax.nn.sigmoid(gb) * pb).astype(BF16)


def _epi_plain(accs, extra, outs):
    outs[0][...] = accs[0].astype(outs[0].dtype)


def _epi_swiglu(accs, extra, outs):
    g, u = accs
    outs[0][...] = (g * jax.nn.sigmoid(g) * u).astype(BF16)


def _mmk_kernel(a_ref, w_ref, o_ref):
    @pl.when(pl.program_id(2) == 0)
    def _():
        o_ref[...] = jnp.zeros_like(o_ref)

    o_ref[...] += jnp.dot(a_ref[...], w_ref[...], preferred_element_type=F32)


def _matmul_ktiled(a, w, *, tm, tn, tk):
    M, K = a.shape
    N = w.shape[1]
    return pl.pallas_call(
        _mmk_kernel,
        grid=(M // tm, N // tn, K // tk),
        in_specs=[pl.BlockSpec((tm, tk), lambda i, j, k: (i, k)),
                  pl.BlockSpec((tk, tn), lambda i, j, k: (k, j))],
        out_specs=pl.BlockSpec((tm, tn), lambda i, j, k: (i, j)),
        out_shape=jax.ShapeDtypeStruct((M, N), F32),
        compiler_params=_params(("parallel", "parallel", "arbitrary"),
                                4 * (tm + tn) * tk + 4 * tm * tn * 4 + (4 << 20)),
        name="ffn_down",
    )(a, w)


def _cumsum_kernel(x_ref, o_ref, *, width):
    S = x_ref.shape[2]
    r = lax.broadcasted_iota(jnp.int32, (width, width), 0)
    c = lax.broadcasted_iota(jnp.int32, (width, width), 1)
    upper = jnp.where(r <= c, 1.0, 0.0).astype(BF16)
    carry = jnp.zeros((x_ref.shape[1], 1), F32)
    for i in range(S // width):
        x = x_ref[0, :, i * width:(i + 1) * width]
        hi = x.astype(BF16)
        r1 = x - hi.astype(F32)
        mid = r1.astype(BF16)
        lo = (r1 - mid.astype(F32)).astype(BF16)
        y = (jnp.dot(hi, upper, preferred_element_type=F32)
             + jnp.dot(mid, upper, preferred_element_type=F32)
             + jnp.dot(lo, upper, preferred_element_type=F32)) + carry
        o_ref[0, :, i * width:(i + 1) * width] = y
        carry = y[:, width - 1:width]


def _cumsum(x):
    B, H, S = x.shape
    width = _tile(S, 256)
    spec = pl.BlockSpec((1, H, S), lambda b: (b, 0, 0))
    return pl.pallas_call(
        functools.partial(_cumsum_kernel, width=width),
        grid=(B,), in_specs=[spec], out_specs=spec,
        out_shape=jax.ShapeDtypeStruct((B, H, S), F32),
        compiler_params=_params(("parallel",), 16 << 20),
        name="cumsum",
    )(x)


def _fox_kernel(q_ref, k_ref, v_ref, ck_ref, o_ref, m_ref, l_ref, acc_ref, *, tq, scale):
    i = pl.program_id(2)
    q = q_ref[...]
    m_ref[...] = jnp.full(m_ref.shape, NEG, F32)
    l_ref[...] = jnp.zeros(l_ref.shape, F32)
    acc_ref[...] = jnp.zeros(acc_ref.shape, F32)

    def chunk(c, diagonal):
        start = pl.multiple_of(c * tq, tq)
        k = k_ref[pl.ds(start, tq), :]
        v = v_ref[pl.ds(start, tq), :]
        s = lax.dot_general(q, k, (((1,), (1,)), ((), ())), preferred_element_type=F32) * scale
        s = s - ck_ref[0, 0, :, pl.ds(start, tq)]
        if diagonal:
            row = lax.broadcasted_iota(jnp.int32, s.shape, 0)
            col = lax.broadcasted_iota(jnp.int32, s.shape, 1)
            s = jnp.where(col <= row, s, NEG)
        m_prev = m_ref[...]
        m_new = jnp.maximum(m_prev, jnp.max(s, axis=-1, keepdims=True))
        alpha = jnp.exp(m_prev - m_new)
        p = jnp.exp(s - m_new)
        l_ref[...] = alpha * l_ref[...] + jnp.sum(p, axis=-1, keepdims=True)
        acc_ref[...] = alpha * acc_ref[...] + jnp.dot(p.astype(BF16), v, preferred_element_type=F32)
        m_ref[...] = m_new

    def body(c, carry):
        chunk(c, False)
        return carry

    lax.fori_loop(0, i, body, 0)
    chunk(i, True)
    o_ref[...] = (acc_ref[...] / l_ref[...]).astype(BF16)


def _fox(zb, cum, B, S):
    M = B * S
    tq = _tile(S, 512)
    nq = S // tq
    H = FOX_HEADS
    return pl.pallas_call(
        functools.partial(_fox_kernel, tq=tq, scale=HEAD_DIM ** -0.5),
        grid=(B, H, nq),
        in_specs=[pl.BlockSpec((tq, HEAD_DIM), lambda b, h, i: (b * nq + i, h)),
                  pl.BlockSpec((S, HEAD_DIM), lambda b, h, i: (b, H + h)),
                  pl.BlockSpec((S, HEAD_DIM), lambda b, h, i: (b, 2 * H + h)),
                  pl.BlockSpec((1, 1, 1, S), lambda b, h, i: (b, h, 0, 0))],
        out_specs=pl.BlockSpec((tq, HEAD_DIM), lambda b, h, i: (b * nq + i, h)),
        out_shape=jax.ShapeDtypeStruct((M, FOX_WIDTH), BF16),
        scratch_shapes=[pltpu.VMEM((tq, 1), F32), pltpu.VMEM((tq, 1), F32),
                        pltpu.VMEM((tq, HEAD_DIM), F32)],
        compiler_params=_params(("parallel", "parallel", "arbitrary"), 32 << 20),
        name="fox_attn",
    )(zb, zb, zb, cum)


def _dsa_index_kernel(qi_ref, r_ref, w_ref, bias_ref, qs_ref, wb_ref, key_ref, x_ref,
                      *, tq, tc, top_k, n_idx_bits):
    i = pl.program_id(1)
    S = bias_ref.shape[2]
    n_slab = IDX_WIDTH // LANES
    n_chunks = ((i + 1) * tq + tc - 1) // tc

    for p in range(n_slab):
        qs_ref[p * tq:(p + 1) * tq, :] = qi_ref[:, p * LANES:(p + 1) * LANES]
    w = w_ref[0]
    for h in range(IDX_HEADS):
        wb_ref[h] = jnp.broadcast_to(w[:, h:h + 1], (tq, LANES))
    bias_ref[...] = jnp.full(bias_ref.shape, NEG, BF16)

    t = i * tq + lax.broadcasted_iota(jnp.int32, (tq, 1), 0)
    lim = (t // CHUNK + 1) * CHUNK

    def col_of(c, width):
        return c * tc + lax.broadcasted_iota(jnp.int32, (tq, width), 1)

    def score_chunk(c, carry):
        start = pl.multiple_of(c * tc, tc)
        rel = jnp.dot(qs_ref[...], r_ref[0, c], preferred_element_type=F32)
        for half in range(tc // LANES):
            acc = jnp.zeros((tq, LANES), F32)
            for p in range(n_slab):
                for hh in range(2):
                    lo = hh * tc + half * LANES
                    blk = rel[p * tq:(p + 1) * tq, lo:lo + LANES]
                    acc = acc + wb_ref[2 * p + hh] * jnp.maximum(blk, 0.0)
            bits = pltpu.bitcast(acc, jnp.int32)
            key = bits ^ ((bits >> 31) & 0x7FFFFFFF)
            key = jnp.where(acc == 0.0, 0, key)
            col = c * tc + half * LANES + lax.broadcasted_iota(jnp.int32, (tq, LANES), 1)
            key = jnp.where(col < lim, key, INT_MIN)
            key_ref[:, pl.ds(pl.multiple_of(start + half * LANES, LANES), LANES)] = key
        return carry

    lax.fori_loop(0, n_chunks, score_chunk, 0)

    def count(pred):
        def body(c, cnt):
            start = pl.multiple_of(c * tc, tc)
            return cnt + jnp.where(pred(key_ref[:, pl.ds(start, tc)], c), 1, 0)
        cnt = lax.fori_loop(0, n_chunks, body, jnp.zeros((tq, tc), jnp.int32))
        return jnp.sum(cnt, axis=1, keepdims=True)

    def bit_body(it, T):
        cand = T + lax.shift_left(jnp.int32(1), 31 - it)
        n_ge = count(lambda kc, c: kc >= cand)
        return jnp.where(n_ge >= top_k, cand, T)

    T = lax.fori_loop(0, 32, bit_body, jnp.full((tq, 1), INT_MIN, jnp.int32))
    n_ge = count(lambda kc, c: kc >= T)
    n_gt = count(lambda kc, c: kc > T)
    need = top_k - n_gt
    excess = jnp.where((n_ge > top_k) & (T != INT_MIN), 1, 0)
    x_ref[...] = jnp.full((tq, 1), S, jnp.int32)

    @pl.when(jnp.max(excess) > 0)
    def _():
        def x_body(it, X):
            cand = X + lax.shift_left(jnp.int32(1), n_idx_bits - 1 - it)
            n = count(lambda kc, c: (kc == T) & (col_of(c, tc) < cand))
            return jnp.where(n <= need - 1, cand, X)
        x_ref[...] = lax.fori_loop(0, n_idx_bits, x_body, jnp.zeros((tq, 1), jnp.int32))

    X = x_ref[...]

    def write_chunk(c, carry):
        start = pl.multiple_of(c * tc, tc)
        kc = key_ref[:, pl.ds(start, tc)]
        col = col_of(c, tc)
        tie = jnp.where(kc == T, jnp.where(col <= X, 0.0, NEG), NEG)
        sel = jnp.where(kc > T, 0.0, tie)
        bias_ref[0, :, pl.ds(start, tc)] = jnp.where(col < lim, sel, NEG).astype(BF16)
        return carry

    lax.fori_loop(0, n_chunks, write_chunk, 0)


def _dsa_index(zb, r, w_i, B, S, top_k):
    tq, tc = _tile(S, 128), _tile(S, 256)
    nq, nc = S // tq, S // tc
    n_slab = IDX_WIDTH // LANES
    n_idx_bits = max(1, (S - 1).bit_length())
    return pl.pallas_call(
        functools.partial(_dsa_index_kernel, tq=tq, tc=tc, top_k=top_k, n_idx_bits=n_idx_bits),
        grid=(B, nq),
        in_specs=[pl.BlockSpec((tq, IDX_WIDTH), lambda b, i: (b * nq + i, BIG_QI // IDX_WIDTH)),
                  pl.BlockSpec((1, nc, LANES, 2 * tc), lambda b, i: (b, 0, 0, 0)),
                  pl.BlockSpec((1, tq, IDX_HEADS), lambda b, i: (b, i, 0))],
        out_specs=pl.BlockSpec((1, tq, S), lambda b, i: (b, i, 0)),
        out_shape=jax.ShapeDtypeStruct((B, S, S), BF16),
        scratch_shapes=[pltpu.VMEM((n_slab * tq, LANES), BF16),
                        pltpu.VMEM((IDX_HEADS, tq, LANES), F32),
                        pltpu.VMEM((tq, S), jnp.int32),
                        pltpu.VMEM((tq, 1), jnp.int32)],
        compiler_params=_params(("parallel", "arbitrary"), 40 << 20),
        name="dsa_index",
    )(zb, r, w_i)


def _dsa_attn_kernel(q_ref, kt_ref, v_ref, bias_ref, o_ref, qs_ref, m_ref, l_ref, acc_ref,
                     *, tq, tk, scale):
    i = pl.program_id(1)
    H = DSA_HEADS
    n_chunks = ((i + 1) * tq + tk - 1) // tk
    for h in range(H):
        qs_ref[h * tq:(h + 1) * tq, :] = q_ref[:, h * HEAD_DIM:(h + 1) * HEAD_DIM]
    m_ref[...] = jnp.full(m_ref.shape, NEG, F32)
    l_ref[...] = jnp.zeros(l_ref.shape, F32)
    acc_ref[...] = jnp.zeros(acc_ref.shape, F32)

    def body(c, carry):
        start = pl.multiple_of(c * tk, tk)
        s = jnp.dot(qs_ref[...], kt_ref[0, :, pl.ds(start, tk)], preferred_element_type=F32) * scale
        bias = bias_ref[0, :, pl.ds(start, tk)].astype(F32)
        s = (s.reshape(H, tq, tk) + bias[None]).reshape(H * tq, tk)
        m_prev = m_ref[...]
        m_new = jnp.maximum(m_prev, jnp.max(s, axis=-1, keepdims=True))
        alpha = jnp.exp(m_prev - m_new)
        p = jnp.exp(s - m_new)
        l_ref[...] = alpha * l_ref[...] + jnp.sum(p, axis=-1, keepdims=True)
        acc_ref[...] = alpha * acc_ref[...] + jnp.dot(
            p.astype(BF16), v_ref[pl.ds(start, tk), :], preferred_element_type=F32)
        m_ref[...] = m_new
        return carry

    lax.fori_loop(0, n_chunks, body, 0)
    out = acc_ref[...] / l_ref[...]
    for h in range(H):
        o_ref[:, h * HEAD_DIM:(h + 1) * HEAD_DIM] = out[h * tq:(h + 1) * tq].astype(BF16)


def _dsa_attn(zb, kt, kv, bias, B, S):
    M = B * S
    tq, tk = _tile(S, 128), _tile(S, 256)
    nq = S // tq
    return pl.pallas_call(
        functools.partial(_dsa_attn_kernel, tq=tq, tk=tk, scale=HEAD_DIM ** -0.5),
        grid=(B, nq),
        in_specs=[pl.BlockSpec((tq, DSA_WIDTH), lambda b, i: (b * nq + i, BIG_QB // DSA_WIDTH)),
                  pl.BlockSpec((1, HEAD_DIM, S), lambda b, i: (b, 0, 0)),
                  pl.BlockSpec((S, HEAD_DIM), lambda b, i: (b, 1)),
                  pl.BlockSpec((1, tq, S), lambda b, i: (b, i, 0))],
        out_specs=pl.BlockSpec((tq, DSA_WIDTH), lambda b, i: (b * nq + i, 0)),
        out_shape=jax.ShapeDtypeStruct((M, DSA_WIDTH), BF16),
        scratch_shapes=[pltpu.VMEM((DSA_HEADS * tq, HEAD_DIM), BF16),
                        pltpu.VMEM((DSA_HEADS * tq, 1), F32),
                        pltpu.VMEM((DSA_HEADS * tq, 1), F32),
                        pltpu.VMEM((DSA_HEADS * tq, HEAD_DIM), F32)],
        compiler_params=_params(("parallel", "arbitrary"), 40 << 20),
        name="dsa_attn",
    )(zb, kt, kv, bias)


def _rope_tables(positions, dim):
    inv = ROPE_THETA ** (-jnp.arange(0, dim, 2, dtype=F32) / dim)
    ang = positions.astype(F32)[..., None] * inv
    cos, sin = jnp.cos(ang), jnp.sin(ang)
    reps = LANES // dim
    cosf = jnp.tile(jnp.concatenate([cos, cos], axis=-1), (1, 1, reps))
    sinf = jnp.tile(jnp.concatenate([-sin, sin], axis=-1), (1, 1, reps))
    return cosf.reshape(-1, LANES), sinf.reshape(-1, LANES)


def kernel(x, c, positions, w_ada, b_ada, norm_g, w_in, b_forget, w_proj_fox, w_proj_dsa,
           w_gate, w_out, w_ff_gate, w_ff_up, w_ff_down):
    B, S, D = x.shape
    M = B * S
    L = w_in.shape[0]
    F = w_ff_gate.shape[2]
    assert w_in.shape[2] == D_IN and S % CHUNK == 0
    top_k = min(TOPK_MAX, S // 4)
    idx_w_scale = IDX_WIDTH ** -0.5

    tm = _tile(M, 1024)
    tn_in = _tile(DSA_WIDTH, 1024)
    tn_merge = _tile(D, 256)
    tn_out = _tile(D, 1024)
    f_pad = -(-F // 1024) * 1024
    tn_ff = _tile(f_pad, 512)
    tk_down = f_pad // 4 if (f_pad // 4) % LANES == 0 else f_pad

    wb16 = lambda w: w.astype(BF16)
    w_big = wb16(jnp.concatenate([w_in[:, :, OFF_QA:OFF_FA], w_in[:, :, OFF_QB:OFF_KB],
                                  w_in[:, :, OFF_QI:OFF_KI]], axis=2))
    w_small = wb16(jnp.concatenate([w_in[:, :, OFF_KB:OFF_QI], w_in[:, :, OFF_KI:D_IN],
                                    w_in[:, :, OFF_FA:OFF_QB],
                                    jnp.zeros((L, D, SMALL_N - 2 * HEAD_DIM - MISC_FA - FOX_HEADS), F32)],
                                   axis=2))
    w_gate16, w_pf16, w_pd16, w_out16 = wb16(w_gate), wb16(w_proj_fox), wb16(w_proj_dsa), wb16(w_out)
    padf = lambda w, axis: jnp.pad(wb16(w), [(0, 0)] * axis + [(0, f_pad - F)] + [(0, 0)] * (2 - axis))
    w_fg16, w_fu16, w_fd16 = padf(w_ff_gate, 2), padf(w_ff_up, 2), padf(w_ff_down, 1)
    fbias = jnp.zeros((L, 1, LANES), F32).at[:, 0, MISC_FA:MISC_FA + FOX_HEADS].set(b_forget)

    cos_h, sin_h = _rope_tables(positions, HEAD_DIM)
    cos_i, sin_i = _rope_tables(positions, IDX_DIM)
    tab_spec = pl.BlockSpec((tm, LANES), lambda i, j: (i, 0))
    tables = [(t, tab_spec) for t in (cos_h, sin_h, cos_i, sin_i)]

    mods = _ada(c, w_ada, b_ada)
    tc = _tile(S, 256)
    nc = S // tc

    _, h = _norm(x, None, mods[0], mods[0], norm_g[0], norm_g[0],
                 scale_row=1, shift_row=0, gb_row=0)
    for l in range(L):
        h2d = h.reshape(M, D)
        (zb,) = _matmul(
            [h2d], [(0, w_big[l], 0)], tables,
            [(jax.ShapeDtypeStruct((M, BIG_N), BF16), pl.BlockSpec((tm, tn_in), lambda i, j: (i, j)))],
            functools.partial(_epi_in_big, tn=tn_in), tm=tm, tn=tn_in, n_tiles=BIG_N // tn_in,
            name="in_proj")
        kv, misc = _matmul(
            [h2d], [(0, w_small[l], 0)],
            tables + [(fbias[l], pl.BlockSpec((1, LANES), lambda i, j: (0, 0)))],
            [(jax.ShapeDtypeStruct((M, 2 * HEAD_DIM), BF16),
              pl.BlockSpec((tm, 2 * HEAD_DIM), lambda i, j: (i, 0))),
             (jax.ShapeDtypeStruct((M, LANES), F32), pl.BlockSpec((tm, LANES), lambda i, j: (i, 0)))],
            functools.partial(_epi_in_small, idx_w_scale=idx_w_scale), tm=tm, tn=SMALL_N, n_tiles=1,
            name="in_proj_small")
        misc3 = misc.reshape(B, S, LANES)
        ki_t = misc3[:, :, MISC_KI:MISC_KI + IDX_DIM].transpose(0, 2, 1).astype(BF16)
        kc = ki_t.reshape(B, IDX_DIM, nc, tc).transpose(0, 2, 1, 3)
        zc = jnp.zeros_like(kc)
        r = jnp.concatenate([jnp.concatenate([kc, zc], axis=3),
                             jnp.concatenate([zc, kc], axis=3)], axis=2)
        w_i = misc3[:, :, MISC_WI:MISC_WI + IDX_HEADS]
        log_f = misc3[:, :, MISC_FA:MISC_FA + FOX_HEADS].transpose(0, 2, 1)
        cum = _cumsum(log_f).reshape(B, FOX_HEADS, 1, S)
        kb_t = kv[:, :HEAD_DIM].reshape(B, S, HEAD_DIM).transpose(0, 2, 1)

        y_a = _fox(zb, cum, B, S)
        bias = _dsa_index(zb, r, w_i, B, S, top_k)
        y_b = _dsa_attn(zb, kb_t, kv, bias, B, S)

        n_gate_tiles = D // tn_merge
        (mixed,) = _matmul(
            [h2d, y_a, y_b],
            [(0, w_gate16[l], 0), (0, w_gate16[l], n_gate_tiles), (1, w_pf16[l], 0), (2, w_pd16[l], 0)],
            [], [(jax.ShapeDtypeStruct((M, D), BF16), pl.BlockSpec((tm, tn_merge), lambda i, j: (i, j)))],
            _epi_merge, tm=tm, tn=tn_merge, n_tiles=n_gate_tiles, name="gate_merge")
        (yo,) = _matmul(
            [mixed], [(0, w_out16[l], 0)], [],
            [(jax.ShapeDtypeStruct((M, D), F32), pl.BlockSpec((tm, tn_out), lambda i, j: (i, j)))],
            _epi_plain, tm=tm, tn=tn_out, n_tiles=D // tn_out, name="out_proj")
        x, h = _norm(x, yo.reshape(B, S, D), mods[l], mods[l], norm_g[l], norm_g[l],
                     gate_row=2, ga_row=1, scale_row=4, shift_row=3, gb_row=2)

        (t,) = _matmul(
            [h.reshape(M, D)], [(0, w_fg16[l], 0), (0, w_fu16[l], 0)], [],
            [(jax.ShapeDtypeStruct((M, f_pad), BF16), pl.BlockSpec((tm, tn_ff), lambda i, j: (i, j)))],
            _epi_swiglu, tm=tm, tn=tn_ff, n_tiles=f_pad // tn_ff, name="ffn_gate_up")
        ff = _matmul_ktiled(t, w_fd16[l], tm=tm, tn=_tile(D, 1024), tk=tk_down)
        last = l == L - 1
        nl = l if last else l + 1
        x, h = _norm(x, ff.reshape(B, S, D), mods[l], mods[nl], norm_g[l], norm_g[nl],
                     gate_row=5, ga_row=3, scale_row=1, shift_row=0, gb_row=0, emit_h=not last)
    return x
```

```python
import functools

import jax
import jax.numpy as jnp
from jax import lax
from jax.experimental import pallas as pl
from jax.experimental.pallas import tpu as pltpu

F32 = jnp.float32
BF16 = jnp.bfloat16

CHUNK = 64
HEAD_DIM = 128
FOX_HEADS = 16
DSA_HEADS = 16
FOX_WIDTH = FOX_HEADS * HEAD_DIM
DSA_WIDTH = DSA_HEADS * HEAD_DIM
IDX_HEADS = 32
IDX_DIM = 64
IDX_WIDTH = IDX_HEADS * IDX_DIM
TOPK_MAX = 256
ROPE_THETA = 10000.0
EPS = 1e-6
N_MOD = 6

OFF_QA = 0
OFF_FA = 3 * FOX_WIDTH
OFF_QB = OFF_FA + FOX_HEADS
OFF_KB = OFF_QB + DSA_WIDTH
OFF_QI = OFF_KB + 2 * HEAD_DIM
OFF_KI = OFF_QI + IDX_WIDTH
OFF_WI = OFF_KI + IDX_DIM
D_IN = OFF_WI + IDX_HEADS

BIG_QB = 3 * FOX_WIDTH
BIG_QI = BIG_QB + DSA_WIDTH
BIG_N = BIG_QI + IDX_WIDTH
MISC_KI = 0
MISC_WI = IDX_DIM
MISC_FA = IDX_DIM + IDX_HEADS
SMALL_N = 3 * HEAD_DIM

LANES = 128
V7X_VMEM_BYTES = 64 * 1024 * 1024
VMEM_BUDGET = V7X_VMEM_BYTES - 8 * 1024 * 1024

NEG = -1e30
INT_MIN = -(2 ** 31)


def _params(semantics, vmem_bytes):
    return pltpu.CompilerParams(dimension_semantics=semantics,
                                vmem_limit_bytes=int(min(vmem_bytes, VMEM_BUDGET)))


def _tile(n, pref):
    if n <= pref:
        return n
    t = (pref // LANES) * LANES
    while t > LANES and n % t:
        t -= LANES
    assert n % t == 0, (n, pref)
    return t


def _ada_kernel(c_ref, w_ref, b_ref, o_ref):
    c = c_ref[...]
    s = c * jax.nn.sigmoid(c)
    acc = jnp.dot(s.astype(BF16), w_ref[...].astype(BF16), preferred_element_type=F32)
    o_ref[...] = acc[None] + b_ref[...]


def _ada(c, w_ada, b_ada):
    B, D = c.shape
    L = b_ada.shape[0]
    N = w_ada.shape[1]
    rows = 8
    cp = jnp.zeros((rows, D), F32).at[:B].set(c)
    tn = _tile(N, 512)
    out = pl.pallas_call(
        _ada_kernel,
        grid=(N // tn,),
        in_specs=[pl.BlockSpec((rows, D), lambda j: (0, 0)),
                  pl.BlockSpec((D, tn), lambda j: (0, j)),
                  pl.BlockSpec((L, 1, tn), lambda j: (0, 0, j))],
        out_specs=pl.BlockSpec((L, rows, tn), lambda j: (0, 0, j)),
        out_shape=jax.ShapeDtypeStruct((L, rows, N), F32),
        compiler_params=_params(("parallel",), 4 * D * tn * 4 + (8 << 20)),
        name="ada",
    )(cp, w_ada, b_ada.reshape(L, 1, N))
    return out[:, :B].reshape(L, B, N_MOD, D)


def _rms(v, g):
    return v * lax.rsqrt(jnp.mean(v * v, axis=-1, keepdims=True) + EPS) * g


def _norm_kernel(*refs, resid, emit_h, gate_row, ga_row, scale_row, shift_row, gb_row):
    refs = list(refs)
    x_ref = refs.pop(0)
    y_ref = refs.pop(0) if resid else None
    moda_ref, modb_ref, ga_ref, gb_ref = refs[:4]
    outs = refs[4:]
    x = x_ref[0]
    if resid:
        gate = moda_ref[0, gate_row:gate_row + 1, :]
        x = x + gate * _rms(y_ref[0], ga_ref[ga_row:ga_row + 1, :])
        outs.pop(0)[0] = x
    if emit_h:
        scale = modb_ref[0, scale_row:scale_row + 1, :]
        shift = modb_ref[0, shift_row:shift_row + 1, :]
        h = _rms(x, gb_ref[gb_row:gb_row + 1, :]) * (1.0 + scale) + shift
        outs.pop(0)[0] = h.astype(BF16)


def _norm(x, y, moda, modb, ga, gb, *, gate_row=0, ga_row=0, scale_row=0, shift_row=0,
          gb_row=0, emit_h=True):
    B, S, D = x.shape
    resid = y is not None
    ts = _tile(S, 256)
    row = pl.BlockSpec((1, ts, D), lambda b, i: (b, i, 0))
    mod = pl.BlockSpec((1, N_MOD, D), lambda b, i: (b, 0, 0))
    gsp = pl.BlockSpec((4, D), lambda b, i: (0, 0))
    in_specs = [row] + ([row] if resid else []) + [mod, mod, gsp, gsp]
    args = [x] + ([y] if resid else []) + [moda, modb, ga, gb]
    out_shape, out_specs = [], []
    if resid:
        out_shape.append(jax.ShapeDtypeStruct((B, S, D), F32))
        out_specs.append(row)
    if emit_h:
        out_shape.append(jax.ShapeDtypeStruct((B, S, D), BF16))
        out_specs.append(row)
    n_rows_f32 = 1 + 2 * resid + 0.5 * emit_h
    outs = pl.pallas_call(
        functools.partial(_norm_kernel, resid=resid, emit_h=emit_h, gate_row=gate_row,
                          ga_row=ga_row, scale_row=scale_row, shift_row=shift_row,
                          gb_row=gb_row),
        grid=(B, S // ts),
        in_specs=in_specs, out_specs=out_specs, out_shape=out_shape,
        compiler_params=_params(("parallel", "parallel"),
                                2 * n_rows_f32 * ts * D * 4 + 6 * ts * D * 4 + (4 << 20)),
        name="norm",
    )(*args)
    outs = list(outs)
    x_new = outs.pop(0) if resid else None
    h = outs.pop(0) if emit_h else None
    return x_new, h


def _mm_kernel(*refs, n_a, pairs, n_extra, epilogue):
    n_w = len(pairs)
    a_refs = refs[:n_a]
    w_refs = refs[n_a:n_a + n_w]
    extra = refs[n_a + n_w:n_a + n_w + n_extra]
    outs = refs[n_a + n_w + n_extra:]
    accs = [jnp.dot(a_refs[ai][...], w_refs[wi][...], preferred_element_type=F32)
            for wi, ai in enumerate(pairs)]
    epilogue(accs, extra, outs)


def _matmul(a_list, w_list, extras, outs, epilogue, *, tm, tn, n_tiles, name):
    M = a_list[0].shape[0]
    in_specs = [pl.BlockSpec((tm, a.shape[1]), lambda i, j: (i, 0)) for a in a_list]
    vmem = sum(2 * tm * a.shape[1] * a.dtype.itemsize for a in a_list)
    for _, w, off in w_list:
        in_specs.append(pl.BlockSpec((w.shape[0], tn), lambda i, j, off=off: (0, j + off)))
        vmem += 2 * w.shape[0] * tn * w.dtype.itemsize + 2 * tm * tn * 4
    in_specs += [sp for _, sp in extras]
    vmem += sum(2 * tm * LANES * 4 for _ in extras)
    vmem += sum(2 * tm * sp.block_shape[-1] * sd.dtype.itemsize for sd, sp in outs)
    return pl.pallas_call(
        functools.partial(_mm_kernel, n_a=len(a_list), pairs=tuple(ai for ai, _, _ in w_list),
                          n_extra=len(extras), epilogue=epilogue),
        grid=(M // tm, n_tiles),
        in_specs=in_specs,
        out_specs=[sp for _, sp in outs],
        out_shape=[sd for sd, _ in outs],
        compiler_params=_params(("parallel", "arbitrary"), vmem + (4 << 20)),
        name=name,
    )(*a_list, *[w for _, w, _ in w_list], *[e for e, _ in extras])


def _rope_slab(a, cosf, sinf, half):
    if 2 * half == LANES:
        partner = pltpu.roll(a, half, 1)
    else:
        lane = lax.broadcasted_iota(jnp.int32, a.shape, 1)
        partner = jnp.where(lane % (2 * half) < half,
                            pltpu.roll(a, LANES - half, 1), pltpu.roll(a, half, 1))
    return a * cosf + partner * sinf


def _epi_in_big(accs, extra, outs, *, tn):
    cos_h, sin_h, cos_i, sin_i = extra
    (o_ref,) = outs
    acc = accs[0]
    j = pl.program_id(1)
    t_qb, t_qi = BIG_QB // tn, BIG_QI // tn

    @pl.when(j < t_qb)
    def _():
        o_ref[...] = acc.astype(BF16)

    def roped(cos_ref, sin_ref, half):
        cosf, sinf = cos_ref[...], sin_ref[...]
        for s in range(tn // LANES):
            sl = slice(s * LANES, (s + 1) * LANES)
            o_ref[:, sl] = _rope_slab(acc[:, sl], cosf, sinf, half).astype(BF16)

    @pl.when((j >= t_qb) & (j < t_qi))
    def _():
        roped(cos_h, sin_h, HEAD_DIM // 2)

    @pl.when(j >= t_qi)
    def _():
        roped(cos_i, sin_i, IDX_DIM // 2)


def _epi_in_small(accs, extra, outs, *, idx_w_scale):
    cos_h, sin_h, cos_i, sin_i, fbias = extra
    kv_ref, misc_ref = outs
    acc = accs[0]
    kb = _rope_slab(acc[:, 0:LANES], cos_h[...], sin_h[...], HEAD_DIM // 2)
    kv_ref[:, 0:LANES] = kb.astype(BF16)
    kv_ref[:, LANES:2 * LANES] = acc[:, LANES:2 * LANES].astype(BF16)
    a = acc[:, 2 * LANES:3 * LANES]
    ki = _rope_slab(a, cos_i[...], sin_i[...], IDX_DIM // 2)
    f = a + fbias[...]
    log_f = jnp.minimum(f, 0.0) - jnp.log1p(jnp.exp(-jnp.abs(f)))
    lane = lax.broadcasted_iota(jnp.int32, a.shape, 1)
    misc_ref[...] = jnp.where(lane < MISC_WI, ki,
                              jnp.where(lane < MISC_FA, a * idx_w_scale, log_f))


def _epi_merge(accs, extra, outs):
    ga, gb, pa, pb = accs
    outs[0][...] = (jax.nn.sigmoid(ga) * pa + jax.nn.sigmoid(gb) * pb).astype(BF16)


def _epi_plain(accs, extra, outs):
    outs[0][...] = accs[0].astype(outs[0].dtype)


def _epi_swiglu(accs, extra, outs):
    g, u = accs
    outs[0][...] = (g * jax.nn.sigmoid(g) * u).astype(BF16)


def _mmk_kernel(a_ref, w_ref, o_ref):
    @pl.when(pl.program_id(2) == 0)
    def _():
        o_ref[...] = jnp.zeros_like(o_ref)

    o_ref[...] += jnp.dot(a_ref[...], w_ref[...], preferred_element_type=F32)


def _matmul_ktiled(a, w, *, tm, tn, tk):
    M, K = a.shape
    N = w.shape[1]
    return pl.pallas_call(
        _mmk_kernel,
        grid=(M // tm, N // tn, K // tk),
        in_specs=[pl.BlockSpec((tm, tk), lambda i, j, k: (i, k)),
                  pl.BlockSpec((tk, tn), lambda i, j, k: (k, j))],
        out_specs=pl.BlockSpec((tm, tn), lambda i, j, k: (i, j)),
        out_shape=jax.ShapeDtypeStruct((M, N), F32),
        compiler_params=_params(("parallel", "parallel", "arbitrary"),
                                4 * (tm + tn) * tk + 4 * tm * tn * 4 + (4 << 20)),
        name="ffn_down",
    )(a, w)


def _cumsum_kernel(x_ref, o_ref, *, width):
    S = x_ref.shape[2]
    r = lax.broadcasted_iota(jnp.int32, (width, width), 0)
    c = lax.broadcasted_iota(jnp.int32, (width, width), 1)
    upper = jnp.where(r <= c, 1.0, 0.0).astype(BF16)
    carry = jnp.zeros((x_ref.shape[1], 1), F32)
    for i in range(S // width):
        x = x_ref[0, :, i * width:(i + 1) * width]
        hi = x.astype(BF16)
        r1 = x - hi.astype(F32)
        mid = r1.astype(BF16)
        lo = (r1 - mid.astype(F32)).astype(BF16)
        y = (jnp.dot(hi, upper, preferred_element_type=F32)
             + jnp.dot(mid, upper, preferred_element_type=F32)
             + jnp.dot(lo, upper, preferred_element_type=F32)) + carry
        o_ref[0, :, i * width:(i + 1) * width] = y
        carry = y[:, width - 1:width]


def _cumsum(x):
    B, H, S = x.shape
    width = _tile(S, 256)
    spec = pl.BlockSpec((1, H, S), lambda b: (b, 0, 0))
    return pl.pallas_call(
        functools.partial(_cumsum_kernel, width=width),
        grid=(B,), in_specs=[spec], out_specs=spec,
        out_shape=jax.ShapeDtypeStruct((B, H, S), F32),
        compiler_params=_params(("parallel",), 16 << 20),
        name="cumsum",
    )(x)


_NT = (((1,), (1,)), ((), ()))
_TN = (((0,), (0,)), ((), ()))


def _fox_kernel(q_ref, k_ref, v_ref, cum_ref, o_ref, ck_ref, m_ref, l_ref, acc_ref, *, tq, scale):
    h = pl.program_id(1)
    i = pl.program_id(2)
    S = k_ref.shape[0]

    @pl.when(i == 0)
    def _():
        onehot = jnp.where(lax.broadcasted_iota(jnp.int32, (1, FOX_HEADS), 1) == h, 1.0, 0.0)
        for c in range(S // tq):
            col = jnp.sum(cum_ref[0, c * tq:(c + 1) * tq, :] * onehot, axis=1, keepdims=True)
            ck_ref[c * tq:(c + 1) * tq, :] = jnp.broadcast_to(col, (tq, LANES))

    q = q_ref[...]
    m_ref[...] = jnp.full(m_ref.shape, NEG, F32)
    l_ref[...] = jnp.zeros(l_ref.shape, F32)
    acc_ref[...] = jnp.zeros(acc_ref.shape, F32)

    def chunk(c, diagonal):
        start = pl.multiple_of(c * tq, tq)
        k = k_ref[pl.ds(start, tq), :]
        v = v_ref[pl.ds(start, tq), :]
        s = lax.dot_general(k, q, _NT, preferred_element_type=F32) * scale
        s = s - pltpu.repeat(ck_ref[pl.ds(start, tq), :], tq // LANES, axis=1)
        if diagonal:
            key = lax.broadcasted_iota(jnp.int32, s.shape, 0)
            qry = lax.broadcasted_iota(jnp.int32, s.shape, 1)
            s = jnp.where(key <= qry, s, NEG)
        m_prev = m_ref[...]
        m_new = jnp.maximum(m_prev, jnp.max(s, axis=0, keepdims=True))
        alpha = jnp.exp(m_prev - m_new)
        p = jnp.exp(s - m_new)
        l_ref[...] = alpha * l_ref[...] + jnp.sum(p, axis=0, keepdims=True)
        acc_ref[...] = alpha * acc_ref[...] + lax.dot_general(
            v, p.astype(BF16), _TN, preferred_element_type=F32)
        m_ref[...] = m_new

    def body(c, carry):
        chunk(c, False)
        return carry

    lax.fori_loop(0, i, body, 0)
    chunk(i, True)
    o_ref[...] = (acc_ref[...] / l_ref[...]).T.astype(BF16)


def _fox(zb, cum, B, S):
    M = B * S
    tq = _tile(S, 512)
    nq = S // tq
    H = FOX_HEADS
    return pl.pallas_call(
        functools.partial(_fox_kernel, tq=tq, scale=HEAD_DIM ** -0.5),
        grid=(B, H, nq),
        in_specs=[pl.BlockSpec((tq, HEAD_DIM), lambda b, h, i: (b * nq + i, h)),
                  pl.BlockSpec((S, HEAD_DIM), lambda b, h, i: (b, H + h)),
                  pl.BlockSpec((S, HEAD_DIM), lambda b, h, i: (b, 2 * H + h)),
                  pl.BlockSpec((1, S, H), lambda b, h, i: (b, 0, 0))],
        out_specs=pl.BlockSpec((tq, HEAD_DIM), lambda b, h, i: (b * nq + i, h)),
        out_shape=jax.ShapeDtypeStruct((M, FOX_WIDTH), BF16),
        scratch_shapes=[pltpu.VMEM((S, LANES), F32),
                        pltpu.VMEM((1, tq), F32), pltpu.VMEM((1, tq), F32),
                        pltpu.VMEM((HEAD_DIM, tq), F32)],
        compiler_params=_params(("parallel", "parallel", "arbitrary"), 32 << 20),
        name="fox_attn",
    )(zb, zb, zb, cum)


def _dsa_index_kernel(qi_ref, r_ref, w_ref, bias_ref, qs_ref, wb_ref, key_ref, x_ref,
                      *, tq, tc, top_k, n_idx_bits):
    i = pl.program_id(1)
    S = bias_ref.shape[2]
    n_slab = IDX_WIDTH // LANES
    n_chunks = ((i + 1) * tq + tc - 1) // tc

    for p in range(n_slab):
        qs_ref[p * tq:(p + 1) * tq, :] = qi_ref[:, p * LANES:(p + 1) * LANES]
    w = w_ref[0]
    for h in range(IDX_HEADS):
        wb_ref[h] = jnp.broadcast_to(w[:, h:h + 1], (tq, LANES))
    bias_ref[...] = jnp.full(bias_ref.shape, NEG, BF16)

    t = i * tq + lax.broadcasted_iota(jnp.int32, (tq, 1), 0)
    lim = (t // CHUNK + 1) * CHUNK

    def col_of(c, width):
        return c * tc + lax.broadcasted_iota(jnp.int32, (tq, width), 1)

    def score_chunk(c, carry):
        start = pl.multiple_of(c * tc, tc)
        rel = jnp.dot(qs_ref[...], r_ref[0, c], preferred_element_type=F32)
        for half in range(tc // LANES):
            acc = jnp.zeros((tq, LANES), F32)
            for p in range(n_slab):
                for hh in range(2):
                    lo = hh * tc + half * LANES
                    blk = rel[p * tq:(p + 1) * tq, lo:lo + LANES]
                    acc = acc + wb_ref[2 * p + hh] * jnp.maximum(blk, 0.0)
            bits = pltpu.bitcast(acc, jnp.int32)
            key = bits ^ ((bits >> 31) & 0x7FFFFFFF)
            key = jnp.where(acc == 0.0, 0, key)
            col = c * tc + half * LANES + lax.broadcasted_iota(jnp.int32, (tq, LANES), 1)
            key = jnp.where(col < lim, key, INT_MIN)
            key_ref[:, pl.ds(pl.multiple_of(start + half * LANES, LANES), LANES)] = key
        return carry

    lax.fori_loop(0, n_chunks, score_chunk, 0)

    def count(pred):
        def body(c, cnt):
            start = pl.multiple_of(c * tc, tc)
            return cnt + jnp.where(pred(key_ref[:, pl.ds(start, tc)], c), 1, 0)
        cnt = lax.fori_loop(0, n_chunks, body, jnp.zeros((tq, tc), jnp.int32))
        return jnp.sum(cnt, axis=1, keepdims=True)

    def bit_body(it, T):
        cand = T + lax.shift_left(jnp.int32(1), 31 - it)
        n_ge = count(lambda kc, c: kc >= cand)
        return jnp.where(n_ge >= top_k, cand, T)

    T = lax.fori_loop(0, 32, bit_body, jnp.full((tq, 1), INT_MIN, jnp.int32))
    n_ge = count(lambda kc, c: kc >= T)
    n_gt = count(lambda kc, c: kc > T)
    need = top_k - n_gt
    excess = jnp.where((n_ge > top_k) & (T != INT_MIN), 1, 0)
    x_ref[...] = jnp.full((tq, 1), S, jnp.int32)

    @pl.when(jnp.max(excess) > 0)
    def _():
        def x_body(it, X):
            cand = X + lax.shift_left(jnp.int32(1), n_idx_bits - 1 - it)
            n = count(lambda kc, c: (kc == T) & (col_of(c, tc) < cand))
            return jnp.where(n <= need - 1, cand, X)
        x_ref[...] = lax.fori_loop(0, n_idx_bits, x_body, jnp.zeros((tq, 1), jnp.int32))

    X = x_ref[...]

    def write_chunk(c, carry):
        start = pl.multiple_of(c * tc, tc)
        kc = key_ref[:, pl.ds(start, tc)]
        col = col_of(c, tc)
        tie = jnp.where(kc == T, jnp.where(col <= X, 0.0, NEG), NEG)
        sel = jnp.where(kc > T, 0.0, tie)
        bias_ref[0, :, pl.ds(start, tc)] = jnp.where(col < lim, sel, NEG).astype(BF16)
        return carry

    lax.fori_loop(0, n_chunks, write_chunk, 0)


def _dsa_index(zb, r, w_i, B, S, top_k):
    tq, tc = _tile(S, 128), _tile(S, 256)
    nq, nc = S // tq, S // tc
    n_slab = IDX_WIDTH // LANES
    n_idx_bits = max(1, (S - 1).bit_length())
    return pl.pallas_call(
        functools.partial(_dsa_index_kernel, tq=tq, tc=tc, top_k=top_k, n_idx_bits=n_idx_bits),
        grid=(B, nq),
        in_specs=[pl.BlockSpec((tq, IDX_WIDTH), lambda b, i: (b * nq + i, BIG_QI // IDX_WIDTH)),
                  pl.BlockSpec((1, nc, LANES, 2 * tc), lambda b, i: (b, 0, 0, 0)),
                  pl.BlockSpec((1, tq, IDX_HEADS), lambda b, i: (b, i, 0))],
        out_specs=pl.BlockSpec((1, tq, S), lambda b, i: (b, i, 0)),
        out_shape=jax.ShapeDtypeStruct((B, S, S), BF16),
        scratch_shapes=[pltpu.VMEM((n_slab * tq, LANES), BF16),
                        pltpu.VMEM((IDX_HEADS, tq, LANES), F32),
                        pltpu.VMEM((tq, S), jnp.int32),
                        pltpu.VMEM((tq, 1), jnp.int32)],
        compiler_params=_params(("parallel", "arbitrary"), 40 << 20),
        name="dsa_index",
    )(zb, r, w_i)


def _dsa_attn_kernel(q_ref, k_ref, v_ref, bias_ref, o_ref, qt_ref, m_ref, l_ref, acc_ref,
                     *, tq, tk, scale):
    i = pl.program_id(1)
    H = DSA_HEADS
    n_chunks = ((i + 1) * tq + tk - 1) // tk
    for h in range(H):
        qh = q_ref[:, h * HEAD_DIM:(h + 1) * HEAD_DIM].astype(F32)
        qt_ref[:, h * tq:(h + 1) * tq] = qh.T.astype(BF16)
    m_ref[...] = jnp.full(m_ref.shape, NEG, F32)
    l_ref[...] = jnp.zeros(l_ref.shape, F32)
    acc_ref[...] = jnp.zeros(acc_ref.shape, F32)

    def body(c, carry):
        start = pl.multiple_of(c * tk, tk)
        s = jnp.dot(k_ref[pl.ds(start, tk), :], qt_ref[...], preferred_element_type=F32) * scale
        bias = bias_ref[0, :, pl.ds(start, tk)].astype(F32).T
        s = s + pltpu.repeat(bias, H, axis=1)
        m_prev = m_ref[...]
        m_new = jnp.maximum(m_prev, jnp.max(s, axis=0, keepdims=True))
        alpha = jnp.exp(m_prev - m_new)
        p = jnp.exp(s - m_new)
        l_ref[...] = alpha * l_ref[...] + jnp.sum(p, axis=0, keepdims=True)
        acc_ref[...] = alpha * acc_ref[...] + lax.dot_general(
            v_ref[pl.ds(start, tk), :], p.astype(BF16), _TN, preferred_element_type=F32)
        m_ref[...] = m_new
        return carry

    lax.fori_loop(0, n_chunks, body, 0)
    out = acc_ref[...] / l_ref[...]
    for h in range(H):
        o_ref[:, h * HEAD_DIM:(h + 1) * HEAD_DIM] = out[:, h * tq:(h + 1) * tq].T.astype(BF16)


def _dsa_attn(zb, kv, bias, B, S):
    M = B * S
    tq, tk = _tile(S, 128), _tile(S, 256)
    nq = S // tq
    return pl.pallas_call(
        functools.partial(_dsa_attn_kernel, tq=tq, tk=tk, scale=HEAD_DIM ** -0.5),
        grid=(B, nq),
        in_specs=[pl.BlockSpec((tq, DSA_WIDTH), lambda b, i: (b * nq + i, BIG_QB // DSA_WIDTH)),
                  pl.BlockSpec((S, HEAD_DIM), lambda b, i: (b, 0)),
                  pl.BlockSpec((S, HEAD_DIM), lambda b, i: (b, 1)),
                  pl.BlockSpec((1, tq, S), lambda b, i: (b, i, 0))],
        out_specs=pl.BlockSpec((tq, DSA_WIDTH), lambda b, i: (b * nq + i, 0)),
        out_shape=jax.ShapeDtypeStruct((M, DSA_WIDTH), BF16),
        scratch_shapes=[pltpu.VMEM((HEAD_DIM, DSA_HEADS * tq), BF16),
                        pltpu.VMEM((1, DSA_HEADS * tq), F32),
                        pltpu.VMEM((1, DSA_HEADS * tq), F32),
                        pltpu.VMEM((HEAD_DIM, DSA_HEADS * tq), F32)],
        compiler_params=_params(("parallel", "arbitrary"), 40 << 20),
        name="dsa_attn",
    )(zb, kv, kv, bias)


def _rope_tables(positions, dim):
    inv = ROPE_THETA ** (-jnp.arange(0, dim, 2, dtype=F32) / dim)
    ang = positions.astype(F32)[..., None] * inv
    cos, sin = jnp.cos(ang), jnp.sin(ang)
    reps = LANES // dim
    cosf = jnp.tile(jnp.concatenate([cos, cos], axis=-1), (1, 1, reps))
    sinf = jnp.tile(jnp.concatenate([-sin, sin], axis=-1), (1, 1, reps))
    return cosf.reshape(-1, LANES), sinf.reshape(-1, LANES)


def kernel(x, c, positions, w_ada, b_ada, norm_g, w_in, b_forget, w_proj_fox, w_proj_dsa,
           w_gate, w_out, w_ff_gate, w_ff_up, w_ff_down):
    B, S, D = x.shape
    M = B * S
    L = w_in.shape[0]
    F = w_ff_gate.shape[2]
    assert w_in.shape[2] == D_IN and S % CHUNK == 0
    top_k = min(TOPK_MAX, S // 4)
    idx_w_scale = IDX_WIDTH ** -0.5

    tm = _tile(M, 1024)
    tn_in = _tile(DSA_WIDTH, 1024)
    tn_merge = _tile(D, 256)
    tn_out = _tile(D, 1024)
    f_pad = -(-F // 1024) * 1024
    tn_ff = _tile(f_pad, 512)
    tk_down = f_pad // 4 if (f_pad // 4) % LANES == 0 else f_pad

    wb16 = lambda w: w.astype(BF16)
    w_big = wb16(jnp.concatenate([w_in[:, :, OFF_QA:OFF_FA], w_in[:, :, OFF_QB:OFF_KB],
                                  w_in[:, :, OFF_QI:OFF_KI]], axis=2))
    w_small = wb16(jnp.concatenate([w_in[:, :, OFF_KB:OFF_QI], w_in[:, :, OFF_KI:D_IN],
                                    w_in[:, :, OFF_FA:OFF_QB],
                                    jnp.zeros((L, D, SMALL_N - 2 * HEAD_DIM - MISC_FA - FOX_HEADS), F32)],
                                   axis=2))
    w_gate16, w_pf16, w_pd16, w_out16 = wb16(w_gate), wb16(w_proj_fox), wb16(w_proj_dsa), wb16(w_out)
    padf = lambda w, axis: jnp.pad(wb16(w), [(0, 0)] * axis + [(0, f_pad - F)] + [(0, 0)] * (2 - axis))
    w_fg16, w_fu16, w_fd16 = padf(w_ff_gate, 2), padf(w_ff_up, 2), padf(w_ff_down, 1)
    fbias = jnp.zeros((L, 1, LANES), F32).at[:, 0, MISC_FA:MISC_FA + FOX_HEADS].set(b_forget)

    cos_h, sin_h = _rope_tables(positions, HEAD_DIM)
    cos_i, sin_i = _rope_tables(positions, IDX_DIM)
    tab_spec = pl.BlockSpec((tm, LANES), lambda i, j: (i, 0))
    tables = [(t, tab_spec) for t in (cos_h, sin_h, cos_i, sin_i)]

    mods = _ada(c, w_ada, b_ada)
    tc = _tile(S, 256)
    nc = S // tc

    _, h = _norm(x, None, mods[0], mods[0], norm_g[0], norm_g[0],
                 scale_row=1, shift_row=0, gb_row=0)
    for l in range(L):
        h2d = h.reshape(M, D)
        (zb,) = _matmul(
            [h2d], [(0, w_big[l], 0)], tables,
            [(jax.ShapeDtypeStruct((M, BIG_N), BF16), pl.BlockSpec((tm, tn_in), lambda i, j: (i, j)))],
            functools.partial(_epi_in_big, tn=tn_in), tm=tm, tn=tn_in, n_tiles=BIG_N // tn_in,
            name="in_proj")
        kv, misc = _matmul(
            [h2d], [(0, w_small[l], 0)],
            tables + [(fbias[l], pl.BlockSpec((1, LANES), lambda i, j: (0, 0)))],
            [(jax.ShapeDtypeStruct((M, 2 * HEAD_DIM), BF16),
              pl.BlockSpec((tm, 2 * HEAD_DIM), lambda i, j: (i, 0))),
             (jax.ShapeDtypeStruct((M, LANES), F32), pl.BlockSpec((tm, LANES), lambda i, j: (i, 0)))],
            functools.partial(_epi_in_small, idx_w_scale=idx_w_scale), tm=tm, tn=SMALL_N, n_tiles=1,
            name="in_proj_small")
        misc3 = misc.reshape(B, S, LANES)
        ki_t = misc3[:, :, MISC_KI:MISC_KI + IDX_DIM].transpose(0, 2, 1).astype(BF16)
        kc = ki_t.reshape(B, IDX_DIM, nc, tc).transpose(0, 2, 1, 3)
        zc = jnp.zeros_like(kc)
        r = jnp.concatenate([jnp.concatenate([kc, zc], axis=3),
                             jnp.concatenate([zc, kc], axis=3)], axis=2)
        w_i = misc3[:, :, MISC_WI:MISC_WI + IDX_HEADS]
        log_f = misc3[:, :, MISC_FA:MISC_FA + FOX_HEADS].transpose(0, 2, 1)
        cum = _cumsum(log_f).transpose(0, 2, 1)

        y_a = _fox(zb, cum, B, S)
        bias = _dsa_index(zb, r, w_i, B, S, top_k)
        y_b = _dsa_attn(zb, kv, bias, B, S)

        n_gate_tiles = D // tn_merge
        (mixed,) = _matmul(
            [h2d, y_a, y_b],
            [(0, w_gate16[l], 0), (0, w_gate16[l], n_gate_tiles), (1, w_pf16[l], 0), (2, w_pd16[l], 0)],
            [], [(jax.ShapeDtypeStruct((M, D), BF16), pl.BlockSpec((tm, tn_merge), lambda i, j: (i, j)))],
            _epi_merge, tm=tm, tn=tn_merge, n_tiles=n_gate_tiles, name="gate_merge")
        (yo,) = _matmul(
            [mixed], [(0, w_out16[l], 0)], [],
            [(jax.ShapeDtypeStruct((M, D), F32), pl.BlockSpec((tm, tn_out), lambda i, j: (i, j)))],
            _epi_plain, tm=tm, tn=tn_out, n_tiles=D // tn_out, name="out_proj")
        x, h = _norm(x, yo.reshape(B, S, D), mods[l], mods[l], norm_g[l], norm_g[l],
                     gate_row=2, ga_row=1, scale_row=4, shift_row=3, gb_row=2)

        (t,) = _matmul(
            [h.reshape(M, D)], [(0, w_fg16[l], 0), (0, w_fu16[l], 0)], [],
            [(jax.ShapeDtypeStruct((M, f_pad), BF16), pl.BlockSpec((tm, tn_ff), lambda i, j: (i, j)))],
            _epi_swiglu, tm=tm, tn=tn_ff, n_tiles=f_pad // tn_ff, name="ffn_gate_up")
        ff = _matmul_ktiled(t, w_fd16[l], tm=tm, tn=_tile(D, 1024), tk=tk_down)
        last = l == L - 1
        nl = l if last else l + 1
        x, h = _norm(x, ff.reshape(B, S, D), mods[l], mods[nl], norm_g[l], norm_g[nl],
                     gate_row=5, ga_row=3, scale_row=1, shift_row=0, gb_row=0, emit_h=not last)
    return x
```

```python
import functools

import jax
import jax.numpy as jnp
from jax import lax
from jax.experimental import pallas as pl
from jax.experimental.pallas import tpu as pltpu

F32 = jnp.float32
BF16 = jnp.bfloat16

CHUNK = 64
HEAD_DIM = 128
FOX_HEADS = 16
DSA_HEADS = 16
FOX_WIDTH = FOX_HEADS * HEAD_DIM
DSA_WIDTH = DSA_HEADS * HEAD_DIM
IDX_HEADS = 32
IDX_DIM = 64
IDX_WIDTH = IDX_HEADS * IDX_DIM
TOPK_MAX = 256
ROPE_THETA = 10000.0
EPS = 1e-6
N_MOD = 6

OFF_QA = 0
OFF_FA = 3 * FOX_WIDTH
OFF_QB = OFF_FA + FOX_HEADS
OFF_KB = OFF_QB + DSA_WIDTH
OFF_QI = OFF_KB + 2 * HEAD_DIM
OFF_KI = OFF_QI + IDX_WIDTH
OFF_WI = OFF_KI + IDX_DIM
D_IN = OFF_WI + IDX_HEADS

BIG_QB = 3 * FOX_WIDTH
BIG_QI = BIG_QB + DSA_WIDTH
BIG_N = BIG_QI + IDX_WIDTH
MISC_KI = 0
MISC_WI = IDX_DIM
MISC_FA = IDX_DIM + IDX_HEADS
SMALL_N = 3 * HEAD_DIM

LANES = 128
V7X_VMEM_BYTES = 64 * 1024 * 1024
VMEM_BUDGET = V7X_VMEM_BYTES - 8 * 1024 * 1024

NEG = -1e30
INT_MIN = -(2 ** 31)
LOG2E = 1.4426950408889634
LOGIT_SCALE = HEAD_DIM ** -0.5 * LOG2E


def _params(semantics, vmem_bytes):
    return pltpu.CompilerParams(dimension_semantics=semantics,
                                vmem_limit_bytes=int(min(vmem_bytes, VMEM_BUDGET)))


def _tile(n, pref):
    if n <= pref:
        return n
    t = (pref // LANES) * LANES
    while t > LANES and n % t:
        t -= LANES
    assert n % t == 0, (n, pref)
    return t


def _ada_kernel(c_ref, w_ref, b_ref, o_ref):
    c = c_ref[...]
    s = c * jax.nn.sigmoid(c)
    acc = jnp.dot(s.astype(BF16), w_ref[...].astype(BF16), preferred_element_type=F32)
    o_ref[...] = acc[None] + b_ref[...]


def _ada(c, w_ada, b_ada):
    B, D = c.shape
    L = b_ada.shape[0]
    N = w_ada.shape[1]
    rows = 8
    cp = jnp.zeros((rows, D), F32).at[:B].set(c)
    tn = _tile(N, 512)
    out = pl.pallas_call(
        _ada_kernel,
        grid=(N // tn,),
        in_specs=[pl.BlockSpec((rows, D), lambda j: (0, 0)),
                  pl.BlockSpec((D, tn), lambda j: (0, j)),
                  pl.BlockSpec((L, 1, tn), lambda j: (0, 0, j))],
        out_specs=pl.BlockSpec((L, rows, tn), lambda j: (0, 0, j)),
        out_shape=jax.ShapeDtypeStruct((L, rows, N), F32),
        compiler_params=_params(("parallel",), 4 * D * tn * 4 + (8 << 20)),
        name="ada",
    )(cp, w_ada, b_ada.reshape(L, 1, N))
    return out[:, :B].reshape(L, B, N_MOD, D)


def _rms(v, g):
    return v * lax.rsqrt(jnp.mean(v * v, axis=-1, keepdims=True) + EPS) * g


def _norm_kernel(*refs, resid, emit_h, gate_row, ga_row, scale_row, shift_row, gb_row):
    refs = list(refs)
    x_ref = refs.pop(0)
    y_ref = refs.pop(0) if resid else None
    moda_ref, modb_ref, ga_ref, gb_ref = refs[:4]
    outs = refs[4:]
    x = x_ref[0]
    if resid:
        gate = moda_ref[0, gate_row:gate_row + 1, :]
        x = x + gate * _rms(y_ref[0], ga_ref[ga_row:ga_row + 1, :])
        outs.pop(0)[0] = x
    if emit_h:
        scale = modb_ref[0, scale_row:scale_row + 1, :]
        shift = modb_ref[0, shift_row:shift_row + 1, :]
        h = _rms(x, gb_ref[gb_row:gb_row + 1, :]) * (1.0 + scale) + shift
        outs.pop(0)[0] = h.astype(BF16)


def _norm(x, y, moda, modb, ga, gb, *, gate_row=0, ga_row=0, scale_row=0, shift_row=0,
          gb_row=0, emit_h=True):
    B, S, D = x.shape
    resid = y is not None
    ts = _tile(S, 256)
    row = pl.BlockSpec((1, ts, D), lambda b, i: (b, i, 0))
    mod = pl.BlockSpec((1, N_MOD, D), lambda b, i: (b, 0, 0))
    gsp = pl.BlockSpec((4, D), lambda b, i: (0, 0))
    in_specs = [row] + ([row] if resid else []) + [mod, mod, gsp, gsp]
    args = [x] + ([y] if resid else []) + [moda, modb, ga, gb]
    out_shape, out_specs = [], []
    if resid:
        out_shape.append(jax.ShapeDtypeStruct((B, S, D), F32))
        out_specs.append(row)
    if emit_h:
        out_shape.append(jax.ShapeDtypeStruct((B, S, D), BF16))
        out_specs.append(row)
    n_rows_f32 = 1 + 2 * resid + 0.5 * emit_h
    outs = pl.pallas_call(
        functools.partial(_norm_kernel, resid=resid, emit_h=emit_h, gate_row=gate_row,
                          ga_row=ga_row, scale_row=scale_row, shift_row=shift_row,
                          gb_row=gb_row),
        grid=(B, S // ts),
        in_specs=in_specs, out_specs=out_specs, out_shape=out_shape,
        compiler_params=_params(("parallel", "parallel"),
                                2 * n_rows_f32 * ts * D * 4 + 6 * ts * D * 4 + (4 << 20)),
        name="norm",
    )(*args)
    outs = list(outs)
    x_new = outs.pop(0) if resid else None
    h = outs.pop(0) if emit_h else None
    return x_new, h


def _mm_kernel(*refs, n_a, pairs, n_extra, epilogue):
    n_w = len(pairs)
    a_refs = refs[:n_a]
    w_refs = refs[n_a:n_a + n_w]
    extra = refs[n_a + n_w:n_a + n_w + n_extra]
    outs = refs[n_a + n_w + n_extra:]
    accs = [jnp.dot(a_refs[ai][...], w_refs[wi][...], preferred_element_type=F32)
            for wi, ai in enumerate(pairs)]
    epilogue(accs, extra, outs)


def _matmul(a_list, w_list, extras, outs, epilogue, *, tm, tn, n_tiles, name):
    M = a_list[0].shape[0]
    in_specs = [pl.BlockSpec((tm, a.shape[1]), lambda i, j: (i, 0)) for a in a_list]
    vmem = sum(2 * tm * a.shape[1] * a.dtype.itemsize for a in a_list)
    for _, w, off in w_list:
        in_specs.append(pl.BlockSpec((w.shape[0], tn), lambda i, j, off=off: (0, j + off)))
        vmem += 2 * w.shape[0] * tn * w.dtype.itemsize + 2 * tm * tn * 4
    in_specs += [sp for _, sp in extras]
    vmem += sum(2 * tm * LANES * 4 for _ in extras)
    vmem += sum(2 * tm * sp.block_shape[-1] * sd.dtype.itemsize for sd, sp in outs)
    return pl.pallas_call(
        functools.partial(_mm_kernel, n_a=len(a_list), pairs=tuple(ai for ai, _, _ in w_list),
                          n_extra=len(extras), epilogue=epilogue),
        grid=(M // tm, n_tiles),
        in_specs=in_specs,
        out_specs=[sp for _, sp in outs],
        out_shape=[sd for sd, _ in outs],
        compiler_params=_params(("parallel", "arbitrary"), vmem + (4 << 20)),
        name=name,
    )(*a_list, *[w for _, w, _ in w_list], *[e for e, _ in extras])


def _rope_slab(a, cosf, sinf, half):
    if 2 * half == LANES:
        partner = pltpu.roll(a, half, 1)
    else:
        lane = lax.broadcasted_iota(jnp.int32, a.shape, 1)
        partner = jnp.where(lane % (2 * half) < half,
                            pltpu.roll(a, LANES - half, 1), pltpu.roll(a, half, 1))
    return a * cosf + partner * sinf


def _epi_in_big(accs, extra, outs, *, tn):
    cos_h, sin_h, cos_i, sin_i = extra
    (o_ref,) = outs
    acc = accs[0]
    j = pl.program_id(1)
    t_ka, t_qb, t_qi = FOX_WIDTH // tn, BIG_QB // tn, BIG_QI // tn

    @pl.when(j < t_ka)
    def _():
        o_ref[...] = (acc * LOGIT_SCALE).astype(BF16)

    @pl.when((j >= t_ka) & (j < t_qb))
    def _():
        o_ref[...] = acc.astype(BF16)

    def roped(cos_ref, sin_ref, half):
        cosf, sinf = cos_ref[...], sin_ref[...]
        for s in range(tn // LANES):
            sl = slice(s * LANES, (s + 1) * LANES)
            o_ref[:, sl] = _rope_slab(acc[:, sl], cosf, sinf, half).astype(BF16)

    @pl.when((j >= t_qb) & (j < t_qi))
    def _():
        roped(cos_h, sin_h, HEAD_DIM // 2)

    @pl.when(j >= t_qi)
    def _():
        roped(cos_i, sin_i, IDX_DIM // 2)


def _epi_in_small(accs, extra, outs, *, idx_w_scale):
    cos_h, sin_h, cos_i, sin_i, fbias = extra
    kv_ref, misc_ref = outs
    acc = accs[0]
    kb = _rope_slab(acc[:, 0:LANES], cos_h[...], sin_h[...], HEAD_DIM // 2)
    kv_ref[:, 0:LANES] = kb.astype(BF16)
    kv_ref[:, LANES:2 * LANES] = acc[:, LANES:2 * LANES].astype(BF16)
    a = acc[:, 2 * LANES:3 * LANES]
    ki = _rope_slab(a, cos_i[...], sin_i[...], IDX_DIM // 2)
    f = a + fbias[...]
    log_f = jnp.minimum(f, 0.0) - jnp.log1p(jnp.exp(-jnp.abs(f)))
    lane = lax.broadcasted_iota(jnp.int32, a.shape, 1)
    misc_ref[...] = jnp.where(lane < MISC_WI, ki,
                              jnp.where(lane < MISC_FA, a * idx_w_scale, log_f))


def _epi_merge(accs, extra, outs):
    ga, gb, pa, pb = accs
    outs[0][...] = (jax.nn.sigmoid(ga) * pa + jax.nn.sigmoid(gb) * pb).astype(BF16)


def _epi_plain(accs, extra, outs):
    outs[0][...] = accs[0].astype(outs[0].dtype)


def _epi_swiglu(accs, extra, outs):
    g, u = accs
    outs[0][...] = (g * jax.nn.sigmoid(g) * u).astype(BF16)


def _mmk_kernel(a_ref, w_ref, o_ref):
    @pl.when(pl.program_id(2) == 0)
    def _():
        o_ref[...] = jnp.zeros_like(o_ref)

    o_ref[...] += jnp.dot(a_ref[...], w_ref[...], preferred_element_type=F32)


def _matmul_ktiled(a, w, *, tm, tn, tk):
    M, K = a.shape
    N = w.shape[1]
    return pl.pallas_call(
        _mmk_kernel,
        grid=(M // tm, N // tn, K // tk),
        in_specs=[pl.BlockSpec((tm, tk), lambda i, j, k: (i, k)),
                  pl.BlockSpec((tk, tn), lambda i, j, k: (k, j))],
        out_specs=pl.BlockSpec((tm, tn), lambda i, j, k: (i, j)),
        out_shape=jax.ShapeDtypeStruct((M, N), F32),
        compiler_params=_params(("parallel", "parallel", "arbitrary"),
                                4 * (tm + tn) * tk + 4 * tm * tn * 4 + (4 << 20)),
        name="ffn_down",
    )(a, w)


def _cumsum_kernel(x_ref, o_ref, *, width):
    S = x_ref.shape[2]
    r = lax.broadcasted_iota(jnp.int32, (width, width), 0)
    c = lax.broadcasted_iota(jnp.int32, (width, width), 1)
    upper = jnp.where(r <= c, 1.0, 0.0).astype(BF16)
    carry = jnp.zeros((x_ref.shape[1], 1), F32)
    for i in range(S // width):
        x = x_ref[0, :, i * width:(i + 1) * width]
        hi = x.astype(BF16)
        r1 = x - hi.astype(F32)
        mid = r1.astype(BF16)
        lo = (r1 - mid.astype(F32)).astype(BF16)
        y = (jnp.dot(hi, upper, preferred_element_type=F32)
             + jnp.dot(mid, upper, preferred_element_type=F32)
             + jnp.dot(lo, upper, preferred_element_type=F32)) + carry
        o_ref[0, :, i * width:(i + 1) * width] = y
        carry = y[:, width - 1:width]


def _cumsum(x):
    B, H, S = x.shape
    width = _tile(S, 256)
    spec = pl.BlockSpec((1, H, S), lambda b: (b, 0, 0))
    return pl.pallas_call(
        functools.partial(_cumsum_kernel, width=width),
        grid=(B,), in_specs=[spec], out_specs=spec,
        out_shape=jax.ShapeDtypeStruct((B, H, S), F32),
        compiler_params=_params(("parallel",), 16 << 20),
        name="cumsum",
    )(x)


_NT = (((1,), (1,)), ((), ()))
_TN = (((0,), (0,)), ((), ()))


def _fox_kernel(q_ref, k_ref, v_ref, cum_ref, o_ref, ck_ref, m_ref, l_ref, acc_ref, *, tq, hg):
    g = pl.program_id(1)
    i = pl.program_id(2)
    S = k_ref.shape[0]
    D = HEAD_DIM

    @pl.when(i == 0)
    def _():
        lane = lax.broadcasted_iota(jnp.int32, (1, FOX_HEADS), 1)
        for c in range(S // tq):
            blk = cum_ref[0, c * tq:(c + 1) * tq, :]
            for j in range(hg):
                onehot = jnp.where(lane == g * hg + j, LOG2E, 0.0)
                col = jnp.sum(blk * onehot, axis=1, keepdims=True)
                ck_ref[j, c * tq:(c + 1) * tq, :] = jnp.broadcast_to(col, (tq, LANES))

    m_ref[...] = jnp.full(m_ref.shape, NEG, F32)
    l_ref[...] = jnp.zeros(l_ref.shape, F32)
    acc_ref[...] = jnp.zeros(acc_ref.shape, F32)

    def chunk(c, diagonal):
        start = pl.multiple_of(c * tq, tq)

        def logits(j):
            return lax.dot_general(k_ref[pl.ds(start, tq), j * D:(j + 1) * D],
                                   q_ref[:, j * D:(j + 1) * D], _NT, preferred_element_type=F32)

        s_next = logits(0)
        for j in range(hg):
            s = s_next
            if j + 1 < hg:
                s_next = logits(j + 1)
            v = v_ref[pl.ds(start, tq), j * D:(j + 1) * D]
            s = s - jnp.tile(ck_ref[j, pl.ds(start, tq), :], (1, tq // LANES))
            if diagonal:
                key = lax.broadcasted_iota(jnp.int32, s.shape, 0)
                qry = lax.broadcasted_iota(jnp.int32, s.shape, 1)
                s = jnp.where(key <= qry, s, NEG)
            m_prev = m_ref[j]
            m_new = jnp.maximum(m_prev, jnp.max(s, axis=0, keepdims=True))
            alpha = jnp.exp2(m_prev - m_new)
            p = jnp.exp2(s - m_new)
            l_ref[j] = alpha * l_ref[j] + jnp.sum(p, axis=0, keepdims=True)
            acc_ref[j] = alpha * acc_ref[j] + lax.dot_general(
                v, p.astype(BF16), _TN, preferred_element_type=F32)
            m_ref[j] = m_new

    def body(c, carry):
        chunk(c, False)
        return carry

    lax.fori_loop(0, i, body, 0)
    chunk(i, True)
    for j in range(hg):
        o_ref[:, j * D:(j + 1) * D] = (acc_ref[j] / l_ref[j]).T.astype(BF16)


def _fox(zb, cum, B, S):
    M = B * S
    tq = _tile(S, 512)
    nq = S // tq
    H = FOX_HEADS
    hg = 4
    G = H // hg
    W = hg * HEAD_DIM
    return pl.pallas_call(
        functools.partial(_fox_kernel, tq=tq, hg=hg),
        grid=(B, G, nq),
        in_specs=[pl.BlockSpec((tq, W), lambda b, g, i: (b * nq + i, g)),
                  pl.BlockSpec((S, W), lambda b, g, i: (b, G + g)),
                  pl.BlockSpec((S, W), lambda b, g, i: (b, 2 * G + g)),
                  pl.BlockSpec((1, S, H), lambda b, g, i: (b, 0, 0))],
        out_specs=pl.BlockSpec((tq, W), lambda b, g, i: (b * nq + i, g)),
        out_shape=jax.ShapeDtypeStruct((M, FOX_WIDTH), BF16),
        scratch_shapes=[pltpu.VMEM((hg, S, LANES), F32),
                        pltpu.VMEM((hg, 1, tq), F32), pltpu.VMEM((hg, 1, tq), F32),
                        pltpu.VMEM((hg, HEAD_DIM, tq), F32)],
        compiler_params=_params(("parallel", "parallel", "arbitrary"), 40 << 20),
        name="fox_attn",
    )(zb, zb, zb, cum)


def _dsa_index_kernel(qi_ref, r_ref, w_ref, bias_ref, qs_ref, wb_ref, key_ref, x_ref,
                      *, tq, tc, top_k, n_idx_bits):
    i = pl.program_id(1)
    S = bias_ref.shape[2]
    n_slab = IDX_WIDTH // LANES
    n_chunks = ((i + 1) * tq + tc - 1) // tc

    for p in range(n_slab):
        qs_ref[p * tq:(p + 1) * tq, :] = qi_ref[:, p * LANES:(p + 1) * LANES]
    w = w_ref[0]
    for h in range(IDX_HEADS):
        wb_ref[h] = jnp.broadcast_to(w[:, h:h + 1], (tq, LANES))
    bias_ref[...] = jnp.full(bias_ref.shape, NEG, BF16)

    t = i * tq + lax.broadcasted_iota(jnp.int32, (tq, 1), 0)
    lim = (t // CHUNK + 1) * CHUNK

    def col_of(c, width):
        return c * tc + lax.broadcasted_iota(jnp.int32, (tq, width), 1)

    def score_chunk(c, carry):
        start = pl.multiple_of(c * tc, tc)
        rel = jnp.dot(qs_ref[...], r_ref[0, c], preferred_element_type=F32)
        for half in range(tc // LANES):
            acc = jnp.zeros((tq, LANES), F32)
            for p in range(n_slab):
                for hh in range(2):
                    lo = hh * tc + half * LANES
                    blk = rel[p * tq:(p + 1) * tq, lo:lo + LANES]
                    acc = acc + wb_ref[2 * p + hh] * jnp.maximum(blk, 0.0)
            bits = pltpu.bitcast(acc, jnp.int32)
            key = bits ^ ((bits >> 31) & 0x7FFFFFFF)
            key = jnp.where(acc == 0.0, 0, key)
            col = c * tc + half * LANES + lax.broadcasted_iota(jnp.int32, (tq, LANES), 1)
            key = jnp.where(col < lim, key, INT_MIN)
            key_ref[:, pl.ds(pl.multiple_of(start + half * LANES, LANES), LANES)] = key
        return carry

    lax.fori_loop(0, n_chunks, score_chunk, 0)

    def count(pred):
        def body(c, cnt):
            start = pl.multiple_of(c * tc, tc)
            return cnt + jnp.where(pred(key_ref[:, pl.ds(start, tc)], c), 1, 0)
        cnt = lax.fori_loop(0, n_chunks, body, jnp.zeros((tq, tc), jnp.int32))
        return jnp.sum(cnt, axis=1, keepdims=True)

    def bit_body(it, T):
        cand = T + lax.shift_left(jnp.int32(1), 31 - it)
        n_ge = count(lambda kc, c: kc >= cand)
        return jnp.where(n_ge >= top_k, cand, T)

    T = lax.fori_loop(0, 32, bit_body, jnp.full((tq, 1), INT_MIN, jnp.int32))
    n_ge = count(lambda kc, c: kc >= T)
    n_gt = count(lambda kc, c: kc > T)
    need = top_k - n_gt
    excess = jnp.where((n_ge > top_k) & (T != INT_MIN), 1, 0)
    x_ref[...] = jnp.full((tq, 1), S, jnp.int32)

    @pl.when(jnp.max(excess) > 0)
    def _():
        def x_body(it, X):
            cand = X + lax.shift_left(jnp.int32(1), n_idx_bits - 1 - it)
            n = count(lambda kc, c: (kc == T) & (col_of(c, tc) < cand))
            return jnp.where(n <= need - 1, cand, X)
        x_ref[...] = lax.fori_loop(0, n_idx_bits, x_body, jnp.zeros((tq, 1), jnp.int32))

    X = x_ref[...]

    def write_chunk(c, carry):
        start = pl.multiple_of(c * tc, tc)
        kc = key_ref[:, pl.ds(start, tc)]
        col = col_of(c, tc)
        tie = jnp.where(kc == T, jnp.where(col <= X, 0.0, NEG), NEG)
        sel = jnp.where(kc > T, 0.0, tie)
        bias_ref[0, :, pl.ds(start, tc)] = jnp.where(col < lim, sel, NEG).astype(BF16)
        return carry

    lax.fori_loop(0, n_chunks, write_chunk, 0)


def _dsa_index(zb, r, w_i, B, S, top_k):
    tq, tc = _tile(S, 128), _tile(S, 256)
    nq, nc = S // tq, S // tc
    n_slab = IDX_WIDTH // LANES
    n_idx_bits = max(1, (S - 1).bit_length())
    return pl.pallas_call(
        functools.partial(_dsa_index_kernel, tq=tq, tc=tc, top_k=top_k, n_idx_bits=n_idx_bits),
        grid=(B, nq),
        in_specs=[pl.BlockSpec((tq, IDX_WIDTH), lambda b, i: (b * nq + i, BIG_QI // IDX_WIDTH)),
                  pl.BlockSpec((1, nc, LANES, 2 * tc), lambda b, i: (b, 0, 0, 0)),
                  pl.BlockSpec((1, tq, IDX_HEADS), lambda b, i: (b, i, 0))],
        out_specs=pl.BlockSpec((1, tq, S), lambda b, i: (b, i, 0)),
        out_shape=jax.ShapeDtypeStruct((B, S, S), BF16),
        scratch_shapes=[pltpu.VMEM((n_slab * tq, LANES), BF16),
                        pltpu.VMEM((IDX_HEADS, tq, LANES), F32),
                        pltpu.VMEM((tq, S), jnp.int32),
                        pltpu.VMEM((tq, 1), jnp.int32)],
        compiler_params=_params(("parallel", "arbitrary"), 40 << 20),
        name="dsa_index",
    )(zb, r, w_i)


def _dsa_attn_kernel(q_ref, k_ref, v_ref, bias_ref, o_ref, qt_ref, m_ref, l_ref, acc_ref, s_ref,
                     *, tq, tk, groups):
    i = pl.program_id(1)
    H = DSA_HEADS
    n_chunks = ((i + 1) * tq + tk - 1) // tk
    for h in range(H):
        qh = q_ref[:, h * HEAD_DIM:(h + 1) * HEAD_DIM].astype(F32)
        qt_ref[:, h * tq:(h + 1) * tq] = qh.T.astype(BF16)
    m_ref[...] = jnp.full(m_ref.shape, NEG, F32)
    l_ref[...] = jnp.zeros(l_ref.shape, F32)
    acc_ref[...] = jnp.zeros(acc_ref.shape, F32)

    gw = H * tq // groups

    def body(c, carry):
        start = pl.multiple_of(c * tk, tk)
        k = k_ref[pl.ds(start, tk), :]
        v = v_ref[pl.ds(start, tk), :]
        bias = bias_ref[0, :, pl.ds(start, tk)].astype(F32).T
        bias = jnp.tile(bias, (1, gw // tq))

        for g in range(groups):
            s_ref[g] = jnp.dot(k, qt_ref[:, g * gw:(g + 1) * gw], preferred_element_type=F32)
        for g in range(groups):
            sl = slice(g * gw, (g + 1) * gw)
            s = s_ref[g] + bias
            m_prev = m_ref[:, sl]
            m_new = jnp.maximum(m_prev, jnp.max(s, axis=0, keepdims=True))
            alpha = jnp.exp2(m_prev - m_new)
            p = jnp.exp2(s - m_new)
            l_ref[:, sl] = alpha * l_ref[:, sl] + jnp.sum(p, axis=0, keepdims=True)
            acc_ref[:, sl] = alpha * acc_ref[:, sl] + lax.dot_general(
                v, p.astype(BF16), _TN, preferred_element_type=F32)
            m_ref[:, sl] = m_new
        return carry

    lax.fori_loop(0, n_chunks, body, 0)
    out = acc_ref[...] / l_ref[...]
    for h in range(H):
        o_ref[:, h * HEAD_DIM:(h + 1) * HEAD_DIM] = out[:, h * tq:(h + 1) * tq].T.astype(BF16)


def _dsa_attn(zb, kv, bias, B, S):
    M = B * S
    tq, tk = _tile(S, 128), _tile(S, 256)
    nq = S // tq
    groups = 4
    return pl.pallas_call(
        functools.partial(_dsa_attn_kernel, tq=tq, tk=tk, groups=groups),
        grid=(B, nq),
        in_specs=[pl.BlockSpec((tq, DSA_WIDTH), lambda b, i: (b * nq + i, BIG_QB // DSA_WIDTH)),
                  pl.BlockSpec((S, HEAD_DIM), lambda b, i: (b, 0)),
                  pl.BlockSpec((S, HEAD_DIM), lambda b, i: (b, 1)),
                  pl.BlockSpec((1, tq, S), lambda b, i: (b, i, 0))],
        out_specs=pl.BlockSpec((tq, DSA_WIDTH), lambda b, i: (b * nq + i, 0)),
        out_shape=jax.ShapeDtypeStruct((M, DSA_WIDTH), BF16),
        scratch_shapes=[pltpu.VMEM((HEAD_DIM, DSA_HEADS * tq), BF16),
                        pltpu.VMEM((1, DSA_HEADS * tq), F32),
                        pltpu.VMEM((1, DSA_HEADS * tq), F32),
                        pltpu.VMEM((HEAD_DIM, DSA_HEADS * tq), F32),
                        pltpu.VMEM((groups, tk, DSA_HEADS * tq // groups), F32)],
        compiler_params=_params(("parallel", "arbitrary"), 40 << 20),
        name="dsa_attn",
    )(zb, kv, kv, bias)


def _rope_tables(positions, dim):
    inv = ROPE_THETA ** (-jnp.arange(0, dim, 2, dtype=F32) / dim)
    ang = positions.astype(F32)[..., None] * inv
    cos, sin = jnp.cos(ang), jnp.sin(ang)
    reps = LANES // dim
    cosf = jnp.tile(jnp.concatenate([cos, cos], axis=-1), (1, 1, reps))
    sinf = jnp.tile(jnp.concatenate([-sin, sin], axis=-1), (1, 1, reps))
    return cosf.reshape(-1, LANES), sinf.reshape(-1, LANES)


def kernel(x, c, positions, w_ada, b_ada, norm_g, w_in, b_forget, w_proj_fox, w_proj_dsa,
           w_gate, w_out, w_ff_gate, w_ff_up, w_ff_down):
    B, S, D = x.shape
    M = B * S
    L = w_in.shape[0]
    F = w_ff_gate.shape[2]
    assert w_in.shape[2] == D_IN and S % CHUNK == 0
    top_k = min(TOPK_MAX, S // 4)
    idx_w_scale = IDX_WIDTH ** -0.5

    tm = _tile(M, 1024)
    tn_in = _tile(DSA_WIDTH, 1024)
    tn_merge = _tile(D, 256)
    tn_out = _tile(D, 1024)
    f_pad = -(-F // 1024) * 1024
    tn_ff = _tile(f_pad, 512)
    tk_down = f_pad // 4 if (f_pad // 4) % LANES == 0 else f_pad

    wb16 = lambda w: w.astype(BF16)
    w_big = wb16(jnp.concatenate([w_in[:, :, OFF_QA:OFF_FA], w_in[:, :, OFF_QB:OFF_KB],
                                  w_in[:, :, OFF_QI:OFF_KI]], axis=2))
    w_small = wb16(jnp.concatenate([w_in[:, :, OFF_KB:OFF_QI], w_in[:, :, OFF_KI:D_IN],
                                    w_in[:, :, OFF_FA:OFF_QB],
                                    jnp.zeros((L, D, SMALL_N - 2 * HEAD_DIM - MISC_FA - FOX_HEADS), F32)],
                                   axis=2))
    w_gate16, w_pf16, w_pd16, w_out16 = wb16(w_gate), wb16(w_proj_fox), wb16(w_proj_dsa), wb16(w_out)
    padf = lambda w, axis: jnp.pad(wb16(w), [(0, 0)] * axis + [(0, f_pad - F)] + [(0, 0)] * (2 - axis))
    w_fg16, w_fu16, w_fd16 = padf(w_ff_gate, 2), padf(w_ff_up, 2), padf(w_ff_down, 1)
    fbias = jnp.zeros((L, 1, LANES), F32).at[:, 0, MISC_FA:MISC_FA + FOX_HEADS].set(b_forget)

    cos_h, sin_h = _rope_tables(positions, HEAD_DIM)
    cos_i, sin_i = _rope_tables(positions, IDX_DIM)
    tab_spec = pl.BlockSpec((tm, LANES), lambda i, j: (i, 0))
    tables = [(t, tab_spec) for t in (cos_h, sin_h, cos_i, sin_i)]
    q_tables = [(t, tab_spec) for t in (cos_h * LOGIT_SCALE, sin_h * LOGIT_SCALE, cos_i, sin_i)]

    mods = _ada(c, w_ada, b_ada)
    tc = _tile(S, 256)
    nc = S // tc

    _, h = _norm(x, None, mods[0], mods[0], norm_g[0], norm_g[0],
                 scale_row=1, shift_row=0, gb_row=0)
    for l in range(L):
        h2d = h.reshape(M, D)
        (zb,) = _matmul(
            [h2d], [(0, w_big[l], 0)], q_tables,
            [(jax.ShapeDtypeStruct((M, BIG_N), BF16), pl.BlockSpec((tm, tn_in), lambda i, j: (i, j)))],
            functools.partial(_epi_in_big, tn=tn_in), tm=tm, tn=tn_in, n_tiles=BIG_N // tn_in,
            name="in_proj")
        kv, misc = _matmul(
            [h2d], [(0, w_small[l], 0)],
            tables + [(fbias[l], pl.BlockSpec((1, LANES), lambda i, j: (0, 0)))],
            [(jax.ShapeDtypeStruct((M, 2 * HEAD_DIM), BF16),
              pl.BlockSpec((tm, 2 * HEAD_DIM), lambda i, j: (i, 0))),
             (jax.ShapeDtypeStruct((M, LANES), F32), pl.BlockSpec((tm, LANES), lambda i, j: (i, 0)))],
            functools.partial(_epi_in_small, idx_w_scale=idx_w_scale), tm=tm, tn=SMALL_N, n_tiles=1,
            name="in_proj_small")
        misc3 = misc.reshape(B, S, LANES)
        ki_t = misc3[:, :, MISC_KI:MISC_KI + IDX_DIM].transpose(0, 2, 1).astype(BF16)
        kc = ki_t.reshape(B, IDX_DIM, nc, tc).transpose(0, 2, 1, 3)
        zc = jnp.zeros_like(kc)
        r = jnp.concatenate([jnp.concatenate([kc, zc], axis=3),
                             jnp.concatenate([zc, kc], axis=3)], axis=2)
        w_i = misc3[:, :, MISC_WI:MISC_WI + IDX_HEADS]
        log_f = misc3[:, :, MISC_FA:MISC_FA + FOX_HEADS].transpose(0, 2, 1)
        cum = _cumsum(log_f).transpose(0, 2, 1)

        y_a = _fox(zb, cum, B, S)
        bias = _dsa_index(zb, r, w_i, B, S, top_k)
        y_b = _dsa_attn(zb, kv, bias, B, S)

        n_gate_tiles = D // tn_merge
        (mixed,) = _matmul(
            [h2d, y_a, y_b],
            [(0, w_gate16[l], 0), (0, w_gate16[l], n_gate_tiles), (1, w_pf16[l], 0), (2, w_pd16[l], 0)],
            [], [(jax.ShapeDtypeStruct((M, D), BF16), pl.BlockSpec((tm, tn_merge), lambda i, j: (i, j)))],
            _epi_merge, tm=tm, tn=tn_merge, n_tiles=n_gate_tiles, name="gate_merge")
        (yo,) = _matmul(
            [mixed], [(0, w_out16[l], 0)], [],
            [(jax.ShapeDtypeStruct((M, D), F32), pl.BlockSpec((tm, tn_out), lambda i, j: (i, j)))],
            _epi_plain, tm=tm, tn=tn_out, n_tiles=D // tn_out, name="out_proj")
        x, h = _norm(x, yo.reshape(B, S, D), mods[l], mods[l], norm_g[l], norm_g[l],
                     gate_row=2, ga_row=1, scale_row=4, shift_row=3, gb_row=2)

        (t,) = _matmul(
            [h.reshape(M, D)], [(0, w_fg16[l], 0), (0, w_fu16[l], 0)], [],
            [(jax.ShapeDtypeStruct((M, f_pad), BF16), pl.BlockSpec((tm, tn_ff), lambda i, j: (i, j)))],
            _epi_swiglu, tm=tm, tn=tn_ff, n_tiles=f_pad // tn_ff, name="ffn_gate_up")
        ff = _matmul_ktiled(t, w_fd16[l], tm=tm, tn=_tile(D, 1024), tk=tk_down)
        last = l == L - 1
        nl = l if last else l + 1
        x, h = _norm(x, ff.reshape(B, S, D), mods[l], mods[nl], norm_g[l], norm_g[nl],
                     gate_row=5, ga_row=3, scale_row=1, shift_row=0, gb_row=0, emit_h=not last)
    return x
```

```python
import functools

import jax
import jax.numpy as jnp
from jax import lax
from jax.experimental import pallas as pl
from jax.experimental.pallas import tpu as pltpu

F32 = jnp.float32
BF16 = jnp.bfloat16

CHUNK = 64
HEAD_DIM = 128
FOX_HEADS = 16
DSA_HEADS = 16
FOX_WIDTH = FOX_HEADS * HEAD_DIM
DSA_WIDTH = DSA_HEADS * HEAD_DIM
IDX_HEADS = 32
IDX_DIM = 64
IDX_WIDTH = IDX_HEADS * IDX_DIM
TOPK_MAX = 256
ROPE_THETA = 10000.0
EPS = 1e-6
N_MOD = 6

OFF_QA = 0
OFF_FA = 3 * FOX_WIDTH
OFF_QB = OFF_FA + FOX_HEADS
OFF_KB = OFF_QB + DSA_WIDTH
OFF_QI = OFF_KB + 2 * HEAD_DIM
OFF_KI = OFF_QI + IDX_WIDTH
OFF_WI = OFF_KI + IDX_DIM
D_IN = OFF_WI + IDX_HEADS

BIG_QB = 3 * FOX_WIDTH
BIG_QI = BIG_QB + DSA_WIDTH
BIG_N = BIG_QI + IDX_WIDTH
MISC_KI = 0
MISC_WI = IDX_DIM
MISC_FA = IDX_DIM + IDX_HEADS
SMALL_N = 3 * HEAD_DIM

LANES = 128
V7X_VMEM_BYTES = 64 * 1024 * 1024
VMEM_BUDGET = V7X_VMEM_BYTES - 8 * 1024 * 1024

NEG = -1e30
INT_MIN = -(2 ** 31)
LOG2E = 1.4426950408889634
LOGIT_SCALE = HEAD_DIM ** -0.5 * LOG2E


def _params(semantics, vmem_bytes):
    return pltpu.CompilerParams(dimension_semantics=semantics,
                                vmem_limit_bytes=int(min(vmem_bytes, VMEM_BUDGET)))


def _tile(n, pref):
    if n <= pref:
        return n
    t = (pref // LANES) * LANES
    while t > LANES and n % t:
        t -= LANES
    assert n % t == 0, (n, pref)
    return t


def _ada_kernel(c_ref, w_ref, b_ref, o_ref):
    c = c_ref[...]
    s = c * jax.nn.sigmoid(c)
    acc = jnp.dot(s.astype(BF16), w_ref[...].astype(BF16), preferred_element_type=F32)
    o_ref[...] = acc[None] + b_ref[...]


def _ada(c, w_ada, b_ada):
    B, D = c.shape
    L = b_ada.shape[0]
    N = w_ada.shape[1]
    rows = 8
    cp = jnp.zeros((rows, D), F32).at[:B].set(c)
    tn = _tile(N, 512)
    out = pl.pallas_call(
        _ada_kernel,
        grid=(N // tn,),
        in_specs=[pl.BlockSpec((rows, D), lambda j: (0, 0)),
                  pl.BlockSpec((D, tn), lambda j: (0, j)),
                  pl.BlockSpec((L, 1, tn), lambda j: (0, 0, j))],
        out_specs=pl.BlockSpec((L, rows, tn), lambda j: (0, 0, j)),
        out_shape=jax.ShapeDtypeStruct((L, rows, N), F32),
        compiler_params=_params(("parallel",), 4 * D * tn * 4 + (8 << 20)),
        name="ada",
    )(cp, w_ada, b_ada.reshape(L, 1, N))
    return out[:, :B].reshape(L, B, N_MOD, D)


def _rms(v, g):
    return v * lax.rsqrt(jnp.mean(v * v, axis=-1, keepdims=True) + EPS) * g


def _norm_kernel(*refs, resid, emit_h, gate_row, ga_row, scale_row, shift_row, gb_row):
    refs = list(refs)
    x_ref = refs.pop(0)
    y_ref = refs.pop(0) if resid else None
    moda_ref, modb_ref, ga_ref, gb_ref = refs[:4]
    outs = refs[4:]
    x = x_ref[0]
    if resid:
        gate = moda_ref[0, gate_row:gate_row + 1, :]
        x = x + gate * _rms(y_ref[0], ga_ref[ga_row:ga_row + 1, :])
        outs.pop(0)[0] = x
    if emit_h:
        scale = modb_ref[0, scale_row:scale_row + 1, :]
        shift = modb_ref[0, shift_row:shift_row + 1, :]
        h = _rms(x, gb_ref[gb_row:gb_row + 1, :]) * (1.0 + scale) + shift
        outs.pop(0)[0] = h.astype(BF16)


def _norm(x, y, moda, modb, ga, gb, *, gate_row=0, ga_row=0, scale_row=0, shift_row=0,
          gb_row=0, emit_h=True):
    B, S, D = x.shape
    resid = y is not None
    ts = _tile(S, 256)
    row = pl.BlockSpec((1, ts, D), lambda b, i: (b, i, 0))
    mod = pl.BlockSpec((1, N_MOD, D), lambda b, i: (b, 0, 0))
    gsp = pl.BlockSpec((4, D), lambda b, i: (0, 0))
    in_specs = [row] + ([row] if resid else []) + [mod, mod, gsp, gsp]
    args = [x] + ([y] if resid else []) + [moda, modb, ga, gb]
    out_shape, out_specs = [], []
    if resid:
        out_shape.append(jax.ShapeDtypeStruct((B, S, D), F32))
        out_specs.append(row)
    if emit_h:
        out_shape.append(jax.ShapeDtypeStruct((B, S, D), BF16))
        out_specs.append(row)
    n_rows_f32 = 1 + 2 * resid + 0.5 * emit_h
    outs = pl.pallas_call(
        functools.partial(_norm_kernel, resid=resid, emit_h=emit_h, gate_row=gate_row,
                          ga_row=ga_row, scale_row=scale_row, shift_row=shift_row,
                          gb_row=gb_row),
        grid=(B, S // ts),
        in_specs=in_specs, out_specs=out_specs, out_shape=out_shape,
        compiler_params=_params(("parallel", "parallel"),
                                2 * n_rows_f32 * ts * D * 4 + 6 * ts * D * 4 + (4 << 20)),
        name="norm",
    )(*args)
    outs = list(outs)
    x_new = outs.pop(0) if resid else None
    h = outs.pop(0) if emit_h else None
    return x_new, h


def _mm_kernel(*refs, n_a, pairs, n_extra, epilogue):
    n_w = len(pairs)
    a_refs = refs[:n_a]
    w_refs = refs[n_a:n_a + n_w]
    extra = refs[n_a + n_w:n_a + n_w + n_extra]
    outs = refs[n_a + n_w + n_extra:]
    accs = [jnp.dot(a_refs[ai][...], w_refs[wi][...], preferred_element_type=F32)
            for wi, ai in enumerate(pairs)]
    epilogue(accs, extra, outs)


def _matmul(a_list, w_list, extras, outs, epilogue, *, tm, tn, n_tiles, name):
    M = a_list[0].shape[0]
    in_specs = [pl.BlockSpec((tm, a.shape[1]), lambda i, j: (i, 0)) for a in a_list]
    vmem = sum(2 * tm * a.shape[1] * a.dtype.itemsize for a in a_list)
    for _, w, off in w_list:
        in_specs.append(pl.BlockSpec((w.shape[0], tn), lambda i, j, off=off: (0, j + off)))
        vmem += 2 * w.shape[0] * tn * w.dtype.itemsize + 2 * tm * tn * 4
    in_specs += [sp for _, sp in extras]
    vmem += sum(2 * tm * LANES * 4 for _ in extras)
    vmem += sum(2 * tm * sp.block_shape[-1] * sd.dtype.itemsize for sd, sp in outs)
    return pl.pallas_call(
        functools.partial(_mm_kernel, n_a=len(a_list), pairs=tuple(ai for ai, _, _ in w_list),
                          n_extra=len(extras), epilogue=epilogue),
        grid=(M // tm, n_tiles),
        in_specs=in_specs,
        out_specs=[sp for _, sp in outs],
        out_shape=[sd for sd, _ in outs],
        compiler_params=_params(("parallel", "arbitrary"), vmem + (4 << 20)),
        name=name,
    )(*a_list, *[w for _, w, _ in w_list], *[e for e, _ in extras])


def _rope_slab(a, cosf, sinf, half):
    if 2 * half == LANES:
        partner = pltpu.roll(a, half, 1)
    else:
        lane = lax.broadcasted_iota(jnp.int32, a.shape, 1)
        partner = jnp.where(lane % (2 * half) < half,
                            pltpu.roll(a, LANES - half, 1), pltpu.roll(a, half, 1))
    return a * cosf + partner * sinf


def _epi_in_big(accs, extra, outs, *, tn):
    cos_h, sin_h, cos_i, sin_i = extra
    (o_ref,) = outs
    acc = accs[0]
    j = pl.program_id(1)
    t_ka, t_qb, t_qi = FOX_WIDTH // tn, BIG_QB // tn, BIG_QI // tn

    @pl.when(j < t_ka)
    def _():
        o_ref[...] = (acc * LOGIT_SCALE).astype(BF16)

    @pl.when((j >= t_ka) & (j < t_qb))
    def _():
        o_ref[...] = acc.astype(BF16)

    def roped(cos_ref, sin_ref, half):
        cosf, sinf = cos_ref[...], sin_ref[...]
        for s in range(tn // LANES):
            sl = slice(s * LANES, (s + 1) * LANES)
            o_ref[:, sl] = _rope_slab(acc[:, sl], cosf, sinf, half).astype(BF16)

    @pl.when((j >= t_qb) & (j < t_qi))
    def _():
        roped(cos_h, sin_h, HEAD_DIM // 2)

    @pl.when(j >= t_qi)
    def _():
        roped(cos_i, sin_i, IDX_DIM // 2)


def _epi_in_small(accs, extra, outs, *, idx_w_scale):
    cos_h, sin_h, cos_i, sin_i, fbias = extra
    kv_ref, misc_ref = outs
    acc = accs[0]
    kb = _rope_slab(acc[:, 0:LANES], cos_h[...], sin_h[...], HEAD_DIM // 2)
    kv_ref[:, 0:LANES] = kb.astype(BF16)
    kv_ref[:, LANES:2 * LANES] = acc[:, LANES:2 * LANES].astype(BF16)
    a = acc[:, 2 * LANES:3 * LANES]
    ki = _rope_slab(a, cos_i[...], sin_i[...], IDX_DIM // 2)
    f = a + fbias[...]
    log_f = jnp.minimum(f, 0.0) - jnp.log1p(jnp.exp(-jnp.abs(f)))
    lane = lax.broadcasted_iota(jnp.int32, a.shape, 1)
    misc_ref[...] = jnp.where(lane < MISC_WI, ki,
                              jnp.where(lane < MISC_FA, a * idx_w_scale, log_f))


def _epi_merge(accs, extra, outs):
    ga, gb, pa, pb = accs
    outs[0][...] = (jax.nn.sigmoid(ga) * pa + jax.nn.sigmoid(gb) * pb).astype(BF16)


def _epi_plain(accs, extra, outs):
    outs[0][...] = accs[0].astype(outs[0].dtype)


def _epi_swiglu(accs, extra, outs):
    g, u = accs
    outs[0][...] = (g * jax.nn.sigmoid(g) * u).astype(BF16)


def _mmk_kernel(a_ref, w_ref, o_ref):
    @pl.when(pl.program_id(2) == 0)
    def _():
        o_ref[...] = jnp.zeros_like(o_ref)

    o_ref[...] += jnp.dot(a_ref[...], w_ref[...], preferred_element_type=F32)


def _matmul_ktiled(a, w, *, tm, tn, tk):
    M, K = a.shape
    N = w.shape[1]
    return pl.pallas_call(
        _mmk_kernel,
        grid=(M // tm, N // tn, K // tk),
        in_specs=[pl.BlockSpec((tm, tk), lambda i, j, k: (i, k)),
                  pl.BlockSpec((tk, tn), lambda i, j, k: (k, j))],
        out_specs=pl.BlockSpec((tm, tn), lambda i, j, k: (i, j)),
        out_shape=jax.ShapeDtypeStruct((M, N), F32),
        compiler_params=_params(("parallel", "parallel", "arbitrary"),
                                4 * (tm + tn) * tk + 4 * tm * tn * 4 + (4 << 20)),
        name="ffn_down",
    )(a, w)


def _cumsum_kernel(x_ref, o_ref, *, width):
    S = x_ref.shape[2]
    r = lax.broadcasted_iota(jnp.int32, (width, width), 0)
    c = lax.broadcasted_iota(jnp.int32, (width, width), 1)
    upper = jnp.where(r <= c, 1.0, 0.0).astype(BF16)
    carry = jnp.zeros((x_ref.shape[1], 1), F32)
    for i in range(S // width):
        x = x_ref[0, :, i * width:(i + 1) * width]
        hi = x.astype(BF16)
        r1 = x - hi.astype(F32)
        mid = r1.astype(BF16)
        lo = (r1 - mid.astype(F32)).astype(BF16)
        y = (jnp.dot(hi, upper, preferred_element_type=F32)
             + jnp.dot(mid, upper, preferred_element_type=F32)
             + jnp.dot(lo, upper, preferred_element_type=F32)) + carry
        o_ref[0, :, i * width:(i + 1) * width] = y
        carry = y[:, width - 1:width]


def _cumsum(x):
    B, H, S = x.shape
    width = _tile(S, 256)
    spec = pl.BlockSpec((1, H, S), lambda b: (b, 0, 0))
    return pl.pallas_call(
        functools.partial(_cumsum_kernel, width=width),
        grid=(B,), in_specs=[spec], out_specs=spec,
        out_shape=jax.ShapeDtypeStruct((B, H, S), F32),
        compiler_params=_params(("parallel",), 16 << 20),
        name="cumsum",
    )(x)


_NT = (((1,), (1,)), ((), ()))
_TN = (((0,), (0,)), ((), ()))


def _fox_kernel(q_ref, k_ref, v_ref, cum_ref, o_ref, ck_ref, m_ref, l_ref, acc_ref, *, tq, hg):
    g = pl.program_id(1)
    i = pl.program_id(2)
    S = k_ref.shape[0]
    D = HEAD_DIM

    @pl.when(i == 0)
    def _():
        lane = lax.broadcasted_iota(jnp.int32, (1, FOX_HEADS), 1)
        for c in range(S // tq):
            blk = cum_ref[0, c * tq:(c + 1) * tq, :]
            for j in range(hg):
                onehot = jnp.where(lane == g * hg + j, LOG2E, 0.0)
                col = jnp.sum(blk * onehot, axis=1, keepdims=True)
                ck_ref[j, c * tq:(c + 1) * tq, :] = jnp.broadcast_to(col, (tq, LANES))

    m_ref[...] = jnp.full(m_ref.shape, NEG, F32)
    l_ref[...] = jnp.zeros(l_ref.shape, F32)
    acc_ref[...] = jnp.zeros(acc_ref.shape, F32)

    def chunk(c, diagonal):
        start = pl.multiple_of(c * tq, tq)

        def logits(j):
            return lax.dot_general(k_ref[pl.ds(start, tq), j * D:(j + 1) * D],
                                   q_ref[:, j * D:(j + 1) * D], _NT, preferred_element_type=F32)

        s_next = logits(0)
        for j in range(hg):
            s = s_next
            if j + 1 < hg:
                s_next = logits(j + 1)
            v = v_ref[pl.ds(start, tq), j * D:(j + 1) * D]
            s = s - jnp.tile(ck_ref[j, pl.ds(start, tq), :], (1, tq // LANES))
            if diagonal:
                key = lax.broadcasted_iota(jnp.int32, s.shape, 0)
                qry = lax.broadcasted_iota(jnp.int32, s.shape, 1)
                s = jnp.where(key <= qry, s, NEG)
            m_prev = m_ref[j]
            m_new = jnp.maximum(m_prev, jnp.max(s, axis=0, keepdims=True))
            alpha = jnp.exp2(m_prev - m_new)
            p = jnp.exp2(s - m_new)
            l_ref[j] = alpha * l_ref[j] + jnp.sum(p, axis=0, keepdims=True)
            acc_ref[j] = alpha * acc_ref[j] + lax.dot_general(
                v, p.astype(BF16), _TN, preferred_element_type=F32)
            m_ref[j] = m_new

    def body(c, carry):
        chunk(c, False)
        return carry

    lax.fori_loop(0, i, body, 0)
    chunk(i, True)
    for j in range(hg):
        o_ref[:, j * D:(j + 1) * D] = (acc_ref[j] / l_ref[j]).T.astype(BF16)


def _fox(zb, cum, B, S):
    M = B * S
    tq = _tile(S, 512)
    nq = S // tq
    H = FOX_HEADS
    hg = 4
    G = H // hg
    W = hg * HEAD_DIM
    return pl.pallas_call(
        functools.partial(_fox_kernel, tq=tq, hg=hg),
        grid=(B, G, nq),
        in_specs=[pl.BlockSpec((tq, W), lambda b, g, i: (b * nq + i, g)),
                  pl.BlockSpec((S, W), lambda b, g, i: (b, G + g)),
                  pl.BlockSpec((S, W), lambda b, g, i: (b, 2 * G + g)),
                  pl.BlockSpec((1, S, H), lambda b, g, i: (b, 0, 0))],
        out_specs=pl.BlockSpec((tq, W), lambda b, g, i: (b * nq + i, g)),
        out_shape=jax.ShapeDtypeStruct((M, FOX_WIDTH), BF16),
        scratch_shapes=[pltpu.VMEM((hg, S, LANES), F32),
                        pltpu.VMEM((hg, 1, tq), F32), pltpu.VMEM((hg, 1, tq), F32),
                        pltpu.VMEM((hg, HEAD_DIM, tq), F32)],
        compiler_params=_params(("parallel", "parallel", "arbitrary"), 40 << 20),
        name="fox_attn",
    )(zb, zb, zb, cum)


def _dsa_index_kernel(qi_ref, ki_ref, wt_ref, bias_ref, qt_ref, key_ref, x_ref,
                      *, tq, tc, top_k, n_idx_bits):
    i = pl.program_id(1)
    S = ki_ref.shape[1]
    n_chunks = ((i + 1) * tq + tc - 1) // tc

    for p in range(IDX_WIDTH // LANES):
        blk = qi_ref[:, p * LANES:(p + 1) * LANES].astype(F32).T
        qt_ref[:, (2 * p) * tq:(2 * p + 1) * tq] = blk[:IDX_DIM].astype(BF16)
        qt_ref[:, (2 * p + 1) * tq:(2 * p + 2) * tq] = blk[IDX_DIM:].astype(BF16)
    bias_ref[...] = jnp.full(bias_ref.shape, NEG, BF16)

    t = i * tq + lax.broadcasted_iota(jnp.int32, (1, tq), 1)
    lim = (t // CHUNK + 1) * CHUNK

    def row_of(c):
        return c * tc + lax.broadcasted_iota(jnp.int32, (tc, tq), 0)

    def score_chunk(c, carry):
        start = pl.multiple_of(c * tc, tc)
        rel = jnp.dot(ki_ref[0, pl.ds(start, tc), :], qt_ref[...],
                      preferred_element_type=F32)
        acc = jnp.zeros((tc, tq), F32)
        for h in range(IDX_HEADS):
            acc = acc + wt_ref[0, h:h + 1, :] * jnp.maximum(rel[:, h * tq:(h + 1) * tq], 0.0)
        bits = pltpu.bitcast(acc, jnp.int32)
        key = bits ^ ((bits >> 31) & 0x7FFFFFFF)
        key = jnp.where(acc == 0.0, 0, key)
        key_ref[pl.ds(start, tc), :] = jnp.where(row_of(c) < lim, key, INT_MIN)
        return carry

    lax.fori_loop(0, n_chunks, score_chunk, 0)

    def count(pred):
        def body(c, cnt):
            start = pl.multiple_of(c * tc, tc)
            hit = jnp.where(pred(key_ref[pl.ds(start, tc), :], c), 1, 0)
            return cnt + jnp.sum(hit.reshape(tc // 32, 32, tq), axis=0)
        cnt = lax.fori_loop(0, n_chunks, body, jnp.zeros((32, tq), jnp.int32))
        return jnp.sum(cnt, axis=0, keepdims=True)

    def bit_body(it, T):
        cand = T + lax.shift_left(jnp.int32(1), 31 - it)
        n_ge = count(lambda kc, c: kc >= cand)
        return jnp.where(n_ge >= top_k, cand, T)

    T = lax.fori_loop(0, 32, bit_body, jnp.full((1, tq), INT_MIN, jnp.int32))
    n_ge = count(lambda kc, c: kc >= T)
    n_gt = count(lambda kc, c: kc > T)
    need = top_k - n_gt
    excess = jnp.where((n_ge > top_k) & (T != INT_MIN), 1, 0)
    x_ref[...] = jnp.full((1, tq), S, jnp.int32)

    @pl.when(jnp.max(excess) > 0)
    def _():
        def x_body(it, X):
            cand = X + lax.shift_left(jnp.int32(1), n_idx_bits - 1 - it)
            n = count(lambda kc, c: (kc == T) & (row_of(c) < cand))
            return jnp.where(n <= need - 1, cand, X)
        x_ref[...] = lax.fori_loop(0, n_idx_bits, x_body, jnp.zeros((1, tq), jnp.int32))

    X = x_ref[...]

    def write_chunk(c, carry):
        start = pl.multiple_of(c * tc, tc)
        kc = key_ref[pl.ds(start, tc), :]
        row = row_of(c)
        tie = jnp.where(kc == T, jnp.where(row <= X, 0.0, NEG), NEG)
        sel = jnp.where(kc > T, 0.0, tie)
        bias_ref[0, 0, pl.ds(start, tc), :] = jnp.where(row < lim, sel, NEG).astype(BF16)
        return carry

    lax.fori_loop(0, n_chunks, write_chunk, 0)


def _dsa_index(zb, ki, wt, B, S, top_k):
    tq, tc = _tile(S, 128), _tile(S, 256)
    nq = S // tq
    n_idx_bits = max(1, (S - 1).bit_length())
    return pl.pallas_call(
        functools.partial(_dsa_index_kernel, tq=tq, tc=tc, top_k=top_k, n_idx_bits=n_idx_bits),
        grid=(B, nq),
        in_specs=[pl.BlockSpec((tq, IDX_WIDTH), lambda b, i: (b * nq + i, BIG_QI // IDX_WIDTH)),
                  pl.BlockSpec((1, S, IDX_DIM), lambda b, i: (b, 0, 0)),
                  pl.BlockSpec((1, IDX_HEADS, tq), lambda b, i: (b, 0, i))],
        out_specs=pl.BlockSpec((1, 1, S, tq), lambda b, i: (b, i, 0, 0)),
        out_shape=jax.ShapeDtypeStruct((B, nq, S, tq), BF16),
        scratch_shapes=[pltpu.VMEM((IDX_DIM, IDX_HEADS * tq), BF16),
                        pltpu.VMEM((S, tq), jnp.int32),
                        pltpu.VMEM((1, tq), jnp.int32)],
        compiler_params=_params(("parallel", "arbitrary"), 40 << 20),
        name="dsa_index",
    )(zb, ki, wt)


def _dsa_attn_kernel(q_ref, k_ref, v_ref, bias_ref, o_ref, qt_ref, m_ref, l_ref, acc_ref, s_ref,
                     *, tq, tk, groups):
    i = pl.program_id(1)
    H = DSA_HEADS
    n_chunks = ((i + 1) * tq + tk - 1) // tk
    for h in range(H):
        qh = q_ref[:, h * HEAD_DIM:(h + 1) * HEAD_DIM].astype(F32)
        qt_ref[:, h * tq:(h + 1) * tq] = qh.T.astype(BF16)
    m_ref[...] = jnp.full(m_ref.shape, NEG, F32)
    l_ref[...] = jnp.zeros(l_ref.shape, F32)
    acc_ref[...] = jnp.zeros(acc_ref.shape, F32)

    gw = H * tq // groups

    def body(c, carry):
        start = pl.multiple_of(c * tk, tk)
        k = k_ref[pl.ds(start, tk), :]
        v = v_ref[pl.ds(start, tk), :]
        bias = bias_ref[0, 0, pl.ds(start, tk), :].astype(F32)
        bias = jnp.tile(bias, (1, gw // tq))

        for g in range(groups):
            s_ref[g] = jnp.dot(k, qt_ref[:, g * gw:(g + 1) * gw], preferred_element_type=F32)
        for g in range(groups):
            sl = slice(g * gw, (g + 1) * gw)
            s = s_ref[g] + bias
            m_prev = m_ref[:, sl]
            m_new = jnp.maximum(m_prev, jnp.max(s, axis=0, keepdims=True))
            alpha = jnp.exp2(m_prev - m_new)
            p = jnp.exp2(s - m_new)
            l_ref[:, sl] = alpha * l_ref[:, sl] + jnp.sum(p, axis=0, keepdims=True)
            acc_ref[:, sl] = alpha * acc_ref[:, sl] + lax.dot_general(
                v, p.astype(BF16), _TN, preferred_element_type=F32)
            m_ref[:, sl] = m_new
        return carry

    lax.fori_loop(0, n_chunks, body, 0)
    out = acc_ref[...] / l_ref[...]
    for h in range(H):
        o_ref[:, h * HEAD_DIM:(h + 1) * HEAD_DIM] = out[:, h * tq:(h + 1) * tq].T.astype(BF16)


def _dsa_attn(zb, kv, bias, B, S):
    M = B * S
    tq, tk = _tile(S, 128), _tile(S, 256)
    nq = S // tq
    groups = 4
    return pl.pallas_call(
        functools.partial(_dsa_attn_kernel, tq=tq, tk=tk, groups=groups),
        grid=(B, nq),
        in_specs=[pl.BlockSpec((tq, DSA_WIDTH), lambda b, i: (b * nq + i, BIG_QB // DSA_WIDTH)),
                  pl.BlockSpec((S, HEAD_DIM), lambda b, i: (b, 0)),
                  pl.BlockSpec((S, HEAD_DIM), lambda b, i: (b, 1)),
                  pl.BlockSpec((1, 1, S, tq), lambda b, i: (b, i, 0, 0))],
        out_specs=pl.BlockSpec((tq, DSA_WIDTH), lambda b, i: (b * nq + i, 0)),
        out_shape=jax.ShapeDtypeStruct((M, DSA_WIDTH), BF16),
        scratch_shapes=[pltpu.VMEM((HEAD_DIM, DSA_HEADS * tq), BF16),
                        pltpu.VMEM((1, DSA_HEADS * tq), F32),
                        pltpu.VMEM((1, DSA_HEADS * tq), F32),
                        pltpu.VMEM((HEAD_DIM, DSA_HEADS * tq), F32),
                        pltpu.VMEM((groups, tk, DSA_HEADS * tq // groups), F32)],
        compiler_params=_params(("parallel", "arbitrary"), 40 << 20),
        name="dsa_attn",
    )(zb, kv, kv, bias)


def _rope_tables(positions, dim):
    inv = ROPE_THETA ** (-jnp.arange(0, dim, 2, dtype=F32) / dim)
    ang = positions.astype(F32)[..., None] * inv
    cos, sin = jnp.cos(ang), jnp.sin(ang)
    reps = LANES // dim
    cosf = jnp.tile(jnp.concatenate([cos, cos], axis=-1), (1, 1, reps))
    sinf = jnp.tile(jnp.concatenate([-sin, sin], axis=-1), (1, 1, reps))
    return cosf.reshape(-1, LANES), sinf.reshape(-1, LANES)


def kernel(x, c, positions, w_ada, b_ada, norm_g, w_in, b_forget, w_proj_fox, w_proj_dsa,
           w_gate, w_out, w_ff_gate, w_ff_up, w_ff_down):
    B, S, D = x.shape
    M = B * S
    L = w_in.shape[0]
    F = w_ff_gate.shape[2]
    assert w_in.shape[2] == D_IN and S % CHUNK == 0
    top_k = min(TOPK_MAX, S // 4)
    idx_w_scale = IDX_WIDTH ** -0.5

    tm = _tile(M, 1024)
    tn_in = _tile(DSA_WIDTH, 1024)
    tn_merge = _tile(D, 256)
    tn_out = _tile(D, 1024)
    f_pad = -(-F // 1024) * 1024
    tn_ff = _tile(f_pad, 512)
    tk_down = f_pad // 4 if (f_pad // 4) % LANES == 0 else f_pad

    wb16 = lambda w: w.astype(BF16)
    w_big = wb16(jnp.concatenate([w_in[:, :, OFF_QA:OFF_FA], w_in[:, :, OFF_QB:OFF_KB],
                                  w_in[:, :, OFF_QI:OFF_KI]], axis=2))
    w_small = wb16(jnp.concatenate([w_in[:, :, OFF_KB:OFF_QI], w_in[:, :, OFF_KI:D_IN],
                                    w_in[:, :, OFF_FA:OFF_QB],
                                    jnp.zeros((L, D, SMALL_N - 2 * HEAD_DIM - MISC_FA - FOX_HEADS), F32)],
                                   axis=2))
    w_gate16, w_pf16, w_pd16, w_out16 = wb16(w_gate), wb16(w_proj_fox), wb16(w_proj_dsa), wb16(w_out)
    padf = lambda w, axis: jnp.pad(wb16(w), [(0, 0)] * axis + [(0, f_pad - F)] + [(0, 0)] * (2 - axis))
    w_fg16, w_fu16, w_fd16 = padf(w_ff_gate, 2), padf(w_ff_up, 2), padf(w_ff_down, 1)
    fbias = jnp.zeros((L, 1, LANES), F32).at[:, 0, MISC_FA:MISC_FA + FOX_HEADS].set(b_forget)

    cos_h, sin_h = _rope_tables(positions, HEAD_DIM)
    cos_i, sin_i = _rope_tables(positions, IDX_DIM)
    tab_spec = pl.BlockSpec((tm, LANES), lambda i, j: (i, 0))
    tables = [(t, tab_spec) for t in (cos_h, sin_h, cos_i, sin_i)]
    q_tables = [(t, tab_spec) for t in (cos_h * LOGIT_SCALE, sin_h * LOGIT_SCALE, cos_i, sin_i)]

    mods = _ada(c, w_ada, b_ada)

    _, h = _norm(x, None, mods[0], mods[0], norm_g[0], norm_g[0],
                 scale_row=1, shift_row=0, gb_row=0)
    for l in range(L):
        h2d = h.reshape(M, D)
        (zb,) = _matmul(
            [h2d], [(0, w_big[l], 0)], q_tables,
            [(jax.ShapeDtypeStruct((M, BIG_N), BF16), pl.BlockSpec((tm, tn_in), lambda i, j: (i, j)))],
            functools.partial(_epi_in_big, tn=tn_in), tm=tm, tn=tn_in, n_tiles=BIG_N // tn_in,
            name="in_proj")
        kv, misc = _matmul(
            [h2d], [(0, w_small[l], 0)],
            tables + [(fbias[l], pl.BlockSpec((1, LANES), lambda i, j: (0, 0)))],
            [(jax.ShapeDtypeStruct((M, 2 * HEAD_DIM), BF16),
              pl.BlockSpec((tm, 2 * HEAD_DIM), lambda i, j: (i, 0))),
             (jax.ShapeDtypeStruct((M, LANES), F32), pl.BlockSpec((tm, LANES), lambda i, j: (i, 0)))],
            functools.partial(_epi_in_small, idx_w_scale=idx_w_scale), tm=tm, tn=SMALL_N, n_tiles=1,
            name="in_proj_small")
        misc3 = misc.reshape(B, S, LANES)
        ki = misc3[:, :, MISC_KI:MISC_KI + IDX_DIM].astype(BF16)
        w_it = misc3[:, :, MISC_WI:MISC_WI + IDX_HEADS].transpose(0, 2, 1)
        log_f = misc3[:, :, MISC_FA:MISC_FA + FOX_HEADS].transpose(0, 2, 1)
        cum = _cumsum(log_f).transpose(0, 2, 1)

        y_a = _fox(zb, cum, B, S)
        bias = _dsa_index(zb, ki, w_it, B, S, top_k)
        y_b = _dsa_attn(zb, kv, bias, B, S)

        n_gate_tiles = D // tn_merge
        (mixed,) = _matmul(
            [h2d, y_a, y_b],
            [(0, w_gate16[l], 0), (0, w_gate16[l], n_gate_tiles), (1, w_pf16[l], 0), (2, w_pd16[l], 0)],
            [], [(jax.ShapeDtypeStruct((M, D), BF16), pl.BlockSpec((tm, tn_merge), lambda i, j: (i, j)))],
            _epi_merge, tm=tm, tn=tn_merge, n_tiles=n_gate_tiles, name="gate_merge")
        (yo,) = _matmul(
            [mixed], [(0, w_out16[l], 0)], [],
            [(jax.ShapeDtypeStruct((M, D), F32), pl.BlockSpec((tm, tn_out), lambda i, j: (i, j)))],
            _epi_plain, tm=tm, tn=tn_out, n_tiles=D // tn_out, name="out_proj")
        x, h = _norm(x, yo.reshape(B, S, D), mods[l], mods[l], norm_g[l], norm_g[l],
                     gate_row=2, ga_row=1, scale_row=4, shift_row=3, gb_row=2)

        (t,) = _matmul(
            [h.reshape(M, D)], [(0, w_fg16[l], 0), (0, w_fu16[l], 0)], [],
            [(jax.ShapeDtypeStruct((M, f_pad), BF16), pl.BlockSpec((tm, tn_ff), lambda i, j: (i, j)))],
            _epi_swiglu, tm=tm, tn=tn_ff, n_tiles=f_pad // tn_ff, name="ffn_gate_up")
        ff = _matmul_ktiled(t, w_fd16[l], tm=tm, tn=_tile(D, 1024), tk=tk_down)
        last = l == L - 1
        nl = l if last else l + 1
        x, h = _norm(x, ff.reshape(B, S, D), mods[l], mods[nl], norm_g[l], norm_g[nl],
                     gate_row=5, ga_row=3, scale_row=1, shift_row=0, gb_row=0, emit_h=not last)
    return x
```

```python
import functools

import jax
import jax.numpy as jnp
from jax import lax
from jax.experimental import pallas as pl
from jax.experimental.pallas import tpu as pltpu

F32 = jnp.float32
BF16 = jnp.bfloat16

CHUNK = 64
HEAD_DIM = 128
FOX_HEADS = 16
DSA_HEADS = 16
FOX_WIDTH = FOX_HEADS * HEAD_DIM
DSA_WIDTH = DSA_HEADS * HEAD_DIM
IDX_HEADS = 32
IDX_DIM = 64
IDX_WIDTH = IDX_HEADS * IDX_DIM
TOPK_MAX = 256
ROPE_THETA = 10000.0
EPS = 1e-6
N_MOD = 6

OFF_QA = 0
OFF_FA = 3 * FOX_WIDTH
OFF_QB = OFF_FA + FOX_HEADS
OFF_KB = OFF_QB + DSA_WIDTH
OFF_QI = OFF_KB + 2 * HEAD_DIM
OFF_KI = OFF_QI + IDX_WIDTH
OFF_WI = OFF_KI + IDX_DIM
D_IN = OFF_WI + IDX_HEADS

BIG_QB = 3 * FOX_WIDTH
BIG_QI = BIG_QB + DSA_WIDTH
BIG_N = BIG_QI + IDX_WIDTH
MISC_KI = 0
MISC_WI = IDX_DIM
MISC_FA = IDX_DIM + IDX_HEADS
SMALL_N = 3 * HEAD_DIM

LANES = 128
V7X_VMEM_BYTES = 64 * 1024 * 1024
VMEM_BUDGET = V7X_VMEM_BYTES - 8 * 1024 * 1024

NEG = -1e30
INT_MIN = -(2 ** 31)
LOG2E = 1.4426950408889634
LOGIT_SCALE = HEAD_DIM ** -0.5 * LOG2E


def _params(semantics, vmem_bytes):
    return pltpu.CompilerParams(dimension_semantics=semantics,
                                vmem_limit_bytes=int(min(vmem_bytes, VMEM_BUDGET)))


def _tile(n, pref):
    if n <= pref:
        return n
    t = (pref // LANES) * LANES
    while t > LANES and n % t:
        t -= LANES
    assert n % t == 0, (n, pref)
    return t


def _ada_kernel(c_ref, w_ref, b_ref, o_ref):
    c = c_ref[...]
    s = c * jax.nn.sigmoid(c)
    acc = jnp.dot(s.astype(BF16), w_ref[...].astype(BF16), preferred_element_type=F32)
    o_ref[...] = acc[None] + b_ref[...]


def _ada(c, w_ada, b_ada):
    B, D = c.shape
    L = b_ada.shape[0]
    N = w_ada.shape[1]
    rows = 8
    cp = jnp.zeros((rows, D), F32).at[:B].set(c)
    tn = _tile(N, 512)
    out = pl.pallas_call(
        _ada_kernel,
        grid=(N // tn,),
        in_specs=[pl.BlockSpec((rows, D), lambda j: (0, 0)),
                  pl.BlockSpec((D, tn), lambda j: (0, j)),
                  pl.BlockSpec((L, 1, tn), lambda j: (0, 0, j))],
        out_specs=pl.BlockSpec((L, rows, tn), lambda j: (0, 0, j)),
        out_shape=jax.ShapeDtypeStruct((L, rows, N), F32),
        compiler_params=_params(("parallel",), 4 * D * tn * 4 + (8 << 20)),
        name="ada",
    )(cp, w_ada, b_ada.reshape(L, 1, N))
    return out[:, :B].reshape(L, B, N_MOD, D)


def _rms(v, g):
    return v * lax.rsqrt(jnp.mean(v * v, axis=-1, keepdims=True) + EPS) * g


def _norm_kernel(*refs, resid, emit_h, gate_row, ga_row, scale_row, shift_row, gb_row):
    refs = list(refs)
    x_ref = refs.pop(0)
    y_ref = refs.pop(0) if resid else None
    moda_ref, modb_ref, ga_ref, gb_ref = refs[:4]
    outs = refs[4:]
    x = x_ref[0]
    if resid:
        gate = moda_ref[0, gate_row:gate_row + 1, :]
        x = x + gate * _rms(y_ref[0], ga_ref[ga_row:ga_row + 1, :])
        outs.pop(0)[0] = x
    if emit_h:
        scale = modb_ref[0, scale_row:scale_row + 1, :]
        shift = modb_ref[0, shift_row:shift_row + 1, :]
        h = _rms(x, gb_ref[gb_row:gb_row + 1, :]) * (1.0 + scale) + shift
        outs.pop(0)[0] = h.astype(BF16)


def _norm(x, y, moda, modb, ga, gb, *, gate_row=0, ga_row=0, scale_row=0, shift_row=0,
          gb_row=0, emit_h=True):
    B, S, D = x.shape
    resid = y is not None
    ts = _tile(S, 256)
    row = pl.BlockSpec((1, ts, D), lambda b, i: (b, i, 0))
    mod = pl.BlockSpec((1, N_MOD, D), lambda b, i: (b, 0, 0))
    gsp = pl.BlockSpec((4, D), lambda b, i: (0, 0))
    in_specs = [row] + ([row] if resid else []) + [mod, mod, gsp, gsp]
    args = [x] + ([y] if resid else []) + [moda, modb, ga, gb]
    out_shape, out_specs = [], []
    if resid:
        out_shape.append(jax.ShapeDtypeStruct((B, S, D), F32))
        out_specs.append(row)
    if emit_h:
        out_shape.append(jax.ShapeDtypeStruct((B, S, D), BF16))
        out_specs.append(row)
    n_rows_f32 = 1 + 2 * resid + 0.5 * emit_h
    outs = pl.pallas_call(
        functools.partial(_norm_kernel, resid=resid, emit_h=emit_h, gate_row=gate_row,
                          ga_row=ga_row, scale_row=scale_row, shift_row=shift_row,
                          gb_row=gb_row),
        grid=(B, S // ts),
        in_specs=in_specs, out_specs=out_specs, out_shape=out_shape,
        compiler_params=_params(("parallel", "parallel"),
                                2 * n_rows_f32 * ts * D * 4 + 6 * ts * D * 4 + (4 << 20)),
        name="norm",
    )(*args)
    outs = list(outs)
    x_new = outs.pop(0) if resid else None
    h = outs.pop(0) if emit_h else None
    return x_new, h


def _mm_kernel(*refs, n_a, pairs, n_extra, epilogue):
    n_w = len(pairs)
    a_refs = refs[:n_a]
    w_refs = refs[n_a:n_a + n_w]
    extra = refs[n_a + n_w:n_a + n_w + n_extra]
    outs = refs[n_a + n_w + n_extra:]
    accs = [jnp.dot(a_refs[ai][...], w_refs[wi][...], preferred_element_type=F32)
            for wi, ai in enumerate(pairs)]
    epilogue(accs, extra, outs)


def _matmul(a_list, w_list, extras, outs, epilogue, *, layer, tm, tn, n_tiles, name):
    M = a_list[0].shape[0]
    in_specs = [pl.BlockSpec((tm, a.shape[1]), lambda i, j: (i, 0)) for a in a_list]
    vmem = sum(2 * tm * a.shape[1] * a.dtype.itemsize for a in a_list)
    for _, w, off in w_list:
        in_specs.append(pl.BlockSpec((None, w.shape[1], tn),
                                     lambda i, j, off=off: (layer, 0, j + off)))
        vmem += 2 * w.shape[1] * tn * w.dtype.itemsize + 2 * tm * tn * 4
    in_specs += [sp for _, sp in extras]
    vmem += sum(2 * tm * LANES * 4 for _ in extras)
    vmem += sum(2 * tm * sp.block_shape[-1] * sd.dtype.itemsize for sd, sp in outs)
    return pl.pallas_call(
        functools.partial(_mm_kernel, n_a=len(a_list), pairs=tuple(ai for ai, _, _ in w_list),
                          n_extra=len(extras), epilogue=epilogue),
        grid=(M // tm, n_tiles),
        in_specs=in_specs,
        out_specs=[sp for _, sp in outs],
        out_shape=[sd for sd, _ in outs],
        compiler_params=_params(("parallel", "arbitrary"), vmem + (4 << 20)),
        name=name,
    )(*a_list, *[w for _, w, _ in w_list], *[e for e, _ in extras])


def _rope_slab(a, cosf, sinf, half):
    if 2 * half == LANES:
        partner = pltpu.roll(a, half, 1)
    else:
        lane = lax.broadcasted_iota(jnp.int32, a.shape, 1)
        partner = jnp.where(lane % (2 * half) < half,
                            pltpu.roll(a, LANES - half, 1), pltpu.roll(a, half, 1))
    return a * cosf + partner * sinf


def _epi_in_big(accs, extra, outs, *, tn):
    cos_h, sin_h, cos_i, sin_i = extra
    (o_ref,) = outs
    acc = accs[0]
    j = pl.program_id(1)
    t_ka, t_qb, t_qi = FOX_WIDTH // tn, BIG_QB // tn, BIG_QI // tn

    @pl.when(j < t_ka)
    def _():
        o_ref[...] = (acc * LOGIT_SCALE).astype(BF16)

    @pl.when((j >= t_ka) & (j < t_qb))
    def _():
        o_ref[...] = acc.astype(BF16)

    def roped(cos_ref, sin_ref, half):
        cosf, sinf = cos_ref[...], sin_ref[...]
        for s in range(tn // LANES):
            sl = slice(s * LANES, (s + 1) * LANES)
            o_ref[:, sl] = _rope_slab(acc[:, sl], cosf, sinf, half).astype(BF16)

    @pl.when((j >= t_qb) & (j < t_qi))
    def _():
        roped(cos_h, sin_h, HEAD_DIM // 2)

    @pl.when(j >= t_qi)
    def _():
        roped(cos_i, sin_i, IDX_DIM // 2)


def _epi_in_small(accs, extra, outs, *, idx_w_scale):
    cos_h, sin_h, cos_i, sin_i, fbias = extra
    kv_ref, misc_ref = outs
    acc = accs[0]
    kb = _rope_slab(acc[:, 0:LANES], cos_h[...], sin_h[...], HEAD_DIM // 2)
    kv_ref[:, 0:LANES] = kb.astype(BF16)
    kv_ref[:, LANES:2 * LANES] = acc[:, LANES:2 * LANES].astype(BF16)
    a = acc[:, 2 * LANES:3 * LANES]
    ki = _rope_slab(a, cos_i[...], sin_i[...], IDX_DIM // 2)
    f = a + fbias[...]
    log_f = jnp.minimum(f, 0.0) - jnp.log1p(jnp.exp(-jnp.abs(f)))
    lane = lax.broadcasted_iota(jnp.int32, a.shape, 1)
    misc_ref[...] = jnp.where(lane < MISC_WI, ki,
                              jnp.where(lane < MISC_FA, a * idx_w_scale, log_f))


def _epi_merge(accs, extra, outs):
    ga, gb, pa, pb = accs
    outs[0][...] = (jax.nn.sigmoid(ga) * pa + jax.nn.sigmoid(gb) * pb).astype(BF16)


def _epi_plain(accs, extra, outs):
    outs[0][...] = accs[0].astype(outs[0].dtype)


def _epi_swiglu(accs, extra, outs):
    g, u = accs
    outs[0][...] = (g * jax.nn.sigmoid(g) * u).astype(BF16)


def _mmk_kernel(a_ref, w_ref, o_ref):
    @pl.when(pl.program_id(2) == 0)
    def _():
        o_ref[...] = jnp.zeros_like(o_ref)

    o_ref[...] += jnp.dot(a_ref[...], w_ref[...], preferred_element_type=F32)


def _matmul_ktiled(a, w, *, layer, tm, tn, tk):
    M, K = a.shape
    N = w.shape[2]
    return pl.pallas_call(
        _mmk_kernel,
        grid=(M // tm, N // tn, K // tk),
        in_specs=[pl.BlockSpec((tm, tk), lambda i, j, k: (i, k)),
                  pl.BlockSpec((None, tk, tn), lambda i, j, k: (layer, k, j))],
        out_specs=pl.BlockSpec((tm, tn), lambda i, j, k: (i, j)),
        out_shape=jax.ShapeDtypeStruct((M, N), F32),
        compiler_params=_params(("parallel", "parallel", "arbitrary"),
                                4 * (tm + tn) * tk + 4 * tm * tn * 4 + (4 << 20)),
        name="ffn_down",
    )(a, w)


def _cumsum_kernel(x_ref, o_ref, *, width):
    S = x_ref.shape[2]
    r = lax.broadcasted_iota(jnp.int32, (width, width), 0)
    c = lax.broadcasted_iota(jnp.int32, (width, width), 1)
    upper = jnp.where(r <= c, 1.0, 0.0).astype(BF16)
    carry = jnp.zeros((x_ref.shape[1], 1), F32)
    for i in range(S // width):
        x = x_ref[0, :, i * width:(i + 1) * width]
        hi = x.astype(BF16)
        r1 = x - hi.astype(F32)
        mid = r1.astype(BF16)
        lo = (r1 - mid.astype(F32)).astype(BF16)
        y = (jnp.dot(hi, upper, preferred_element_type=F32)
             + jnp.dot(mid, upper, preferred_element_type=F32)
             + jnp.dot(lo, upper, preferred_element_type=F32)) + carry
        o_ref[0, :, i * width:(i + 1) * width] = y
        carry = y[:, width - 1:width]


def _cumsum(x):
    B, H, S = x.shape
    width = _tile(S, 256)
    spec = pl.BlockSpec((1, H, S), lambda b: (b, 0, 0))
    return pl.pallas_call(
        functools.partial(_cumsum_kernel, width=width),
        grid=(B,), in_specs=[spec], out_specs=spec,
        out_shape=jax.ShapeDtypeStruct((B, H, S), F32),
        compiler_params=_params(("parallel",), 16 << 20),
        name="cumsum",
    )(x)


_NT = (((1,), (1,)), ((), ()))
_TN = (((0,), (0,)), ((), ()))


def _fox_kernel(q_ref, k_ref, v_ref, cum_ref, o_ref, ck_ref, m_ref, l_ref, acc_ref, s0_ref,
                *, tq, hg):
    g = pl.program_id(1)
    i = pl.program_id(2)
    S = k_ref.shape[0]
    D = HEAD_DIM

    @pl.when(i == 0)
    def _():
        lane = lax.broadcasted_iota(jnp.int32, (1, FOX_HEADS), 1)
        for c in range(S // tq):
            blk = cum_ref[0, c * tq:(c + 1) * tq, :]
            for j in range(hg):
                onehot = jnp.where(lane == g * hg + j, LOG2E, 0.0)
                col = jnp.sum(blk * onehot, axis=1, keepdims=True)
                ck_ref[j, c * tq:(c + 1) * tq, :] = jnp.broadcast_to(col, (tq, LANES))

    m_ref[...] = jnp.full(m_ref.shape, NEG, F32)
    l_ref[...] = jnp.zeros(l_ref.shape, F32)
    acc_ref[...] = jnp.zeros(acc_ref.shape, F32)

    def logits(c, j):
        start = pl.multiple_of(c * tq, tq)
        return lax.dot_general(k_ref[pl.ds(start, tq), j * D:(j + 1) * D],
                               q_ref[:, j * D:(j + 1) * D], _NT, preferred_element_type=F32)

    s0_ref[...] = logits(0, 0)

    def chunk(c, diagonal):
        start = pl.multiple_of(c * tq, tq)
        s_next = s0_ref[...]
        for j in range(hg):
            s = s_next
            if j + 1 < hg:
                s_next = logits(c, j + 1)
            elif not diagonal:
                s0_ref[...] = logits(c + 1, 0)
            v = v_ref[pl.ds(start, tq), j * D:(j + 1) * D]
            s = s - jnp.tile(ck_ref[j, pl.ds(start, tq), :], (1, tq // LANES))
            if diagonal:
                key = lax.broadcasted_iota(jnp.int32, s.shape, 0)
                qry = lax.broadcasted_iota(jnp.int32, s.shape, 1)
                s = jnp.where(key <= qry, s, NEG)
            m_prev = m_ref[j]
            m_new = jnp.maximum(m_prev, jnp.max(s, axis=0, keepdims=True))
            alpha = jnp.exp2(m_prev - m_new)
            p = jnp.exp2(s - m_new)
            l_ref[j] = alpha * l_ref[j] + jnp.sum(p, axis=0, keepdims=True)
            acc_ref[j] = alpha * acc_ref[j] + lax.dot_general(
                v, p.astype(BF16), _TN, preferred_element_type=F32)
            m_ref[j] = m_new

    def body(c, carry):
        chunk(c, False)
        return carry

    lax.fori_loop(0, i, body, 0)
    chunk(i, True)
    for j in range(hg):
        o_ref[:, j * D:(j + 1) * D] = (acc_ref[j] / l_ref[j]).T.astype(BF16)


def _fox(zb, cum, B, S):
    M = B * S
    tq = _tile(S, 512)
    nq = S // tq
    H = FOX_HEADS
    hg = 4
    G = H // hg
    W = hg * HEAD_DIM
    return pl.pallas_call(
        functools.partial(_fox_kernel, tq=tq, hg=hg),
        grid=(B, G, nq),
        in_specs=[pl.BlockSpec((tq, W), lambda b, g, i: (b * nq + i, g)),
                  pl.BlockSpec((S, W), lambda b, g, i: (b, G + g)),
                  pl.BlockSpec((S, W), lambda b, g, i: (b, 2 * G + g)),
                  pl.BlockSpec((1, S, H), lambda b, g, i: (b, 0, 0))],
        out_specs=pl.BlockSpec((tq, W), lambda b, g, i: (b * nq + i, g)),
        out_shape=jax.ShapeDtypeStruct((M, FOX_WIDTH), BF16),
        scratch_shapes=[pltpu.VMEM((hg, S, LANES), F32),
                        pltpu.VMEM((hg, 1, tq), F32), pltpu.VMEM((hg, 1, tq), F32),
                        pltpu.VMEM((hg, HEAD_DIM, tq), F32),
                        pltpu.VMEM((tq, tq), F32)],
        compiler_params=_params(("parallel", "parallel", "arbitrary"), 40 << 20),
        name="fox_attn",
    )(zb, zb, zb, cum)


def _dsa_index_kernel(qi_ref, ki_ref, wt_ref, bias_ref, qt_ref, key_ref, x_ref,
                      *, tq, tc, top_k, n_idx_bits):
    i = pl.program_id(1)
    S = ki_ref.shape[1]
    n_chunks = ((i + 1) * tq + tc - 1) // tc

    for p in range(IDX_WIDTH // LANES):
        blk = qi_ref[:, p * LANES:(p + 1) * LANES].astype(F32).T
        qt_ref[:, (2 * p) * tq:(2 * p + 1) * tq] = blk[:IDX_DIM].astype(BF16)
        qt_ref[:, (2 * p + 1) * tq:(2 * p + 2) * tq] = blk[IDX_DIM:].astype(BF16)
    bias_ref[...] = jnp.full(bias_ref.shape, NEG, BF16)

    t = i * tq + lax.broadcasted_iota(jnp.int32, (1, tq), 1)
    lim = (t // CHUNK + 1) * CHUNK

    def row_of(c):
        return c * tc + lax.broadcasted_iota(jnp.int32, (tc, tq), 0)

    def score_chunk(c, carry):
        start = pl.multiple_of(c * tc, tc)
        rel = jnp.dot(ki_ref[0, pl.ds(start, tc), :], qt_ref[...],
                      preferred_element_type=F32)
        acc = jnp.zeros((tc, tq), F32)
        for h in range(IDX_HEADS):
            acc = acc + wt_ref[0, h:h + 1, :] * jnp.maximum(rel[:, h * tq:(h + 1) * tq], 0.0)
        bits = pltpu.bitcast(acc, jnp.int32)
        key = bits ^ ((bits >> 31) & 0x7FFFFFFF)
        key = jnp.where(acc == 0.0, 0, key)
        key_ref[pl.ds(start, tc), :] = jnp.where(row_of(c) < lim, key, INT_MIN)
        return carry

    lax.fori_loop(0, n_chunks, score_chunk, 0)

    pair = 2 if (S // tc) % 2 == 0 else 1
    n_cnt = (n_chunks + pair - 1) // pair
    tcc = pair * tc
    if pair == 2:
        @pl.when(n_chunks % 2 == 1)
        def _():
            key_ref[pl.ds(pl.multiple_of(n_chunks * tc, tc), tc), :] = jnp.full(
                (tc, tq), INT_MIN, jnp.int32)

    def count(*preds):
        def body(c, cnts):
            kc = key_ref[pl.ds(pl.multiple_of(c * tcc, tcc), tcc), :]
            return tuple(cnt + jnp.sum(jnp.where(pred(kc, c), 1, 0).reshape(tcc // 32, 32, tq), axis=0)
                         for cnt, pred in zip(cnts, preds))
        cnts = lax.fori_loop(0, n_cnt, body,
                             tuple(jnp.zeros((32, tq), jnp.int32) for _ in preds))
        return [jnp.sum(cnt, axis=0, keepdims=True) for cnt in cnts]

    def cnt_row(c):
        return c * tcc + lax.broadcasted_iota(jnp.int32, (tcc, tq), 0)

    def bit_body(it, T):
        cand = T + lax.shift_left(jnp.int32(1), 31 - it)
        (n_ge,) = count(lambda kc, c: kc >= cand)
        return jnp.where(n_ge >= top_k, cand, T)

    T = lax.fori_loop(0, 32, bit_body, jnp.full((1, tq), INT_MIN, jnp.int32))
    n_ge, n_gt = count(lambda kc, c: kc >= T, lambda kc, c: kc > T)
    need = top_k - n_gt
    excess = jnp.where((n_ge > top_k) & (T != INT_MIN), 1, 0)
    x_ref[...] = jnp.full((1, tq), S, jnp.int32)

    @pl.when(jnp.max(excess) > 0)
    def _():
        def x_body(it, X):
            cand = X + lax.shift_left(jnp.int32(1), n_idx_bits - 1 - it)
            (n,) = count(lambda kc, c: (kc == T) & (cnt_row(c) < cand))
            return jnp.where(n <= need - 1, cand, X)
        x_ref[...] = lax.fori_loop(0, n_idx_bits, x_body, jnp.zeros((1, tq), jnp.int32))

    X = x_ref[...]

    def write_chunk(c, carry):
        start = pl.multiple_of(c * tc, tc)
        kc = key_ref[pl.ds(start, tc), :]
        row = row_of(c)
        tie = jnp.where(kc == T, jnp.where(row <= X, 0.0, NEG), NEG)
        sel = jnp.where(kc > T, 0.0, tie)
        bias_ref[0, 0, pl.ds(start, tc), :] = jnp.where(row < lim, sel, NEG).astype(BF16)
        return carry

    lax.fori_loop(0, n_chunks, write_chunk, 0)


def _dsa_index(zb, ki, wt, B, S, top_k):
    tq, tc = _tile(S, 128), _tile(S, 256)
    nq = S // tq
    n_idx_bits = max(1, (S - 1).bit_length())
    return pl.pallas_call(
        functools.partial(_dsa_index_kernel, tq=tq, tc=tc, top_k=top_k, n_idx_bits=n_idx_bits),
        grid=(B, nq),
        in_specs=[pl.BlockSpec((tq, IDX_WIDTH), lambda b, i: (b * nq + i, BIG_QI // IDX_WIDTH)),
                  pl.BlockSpec((1, S, IDX_DIM), lambda b, i: (b, 0, 0)),
                  pl.BlockSpec((1, IDX_HEADS, tq), lambda b, i: (b, 0, i))],
        out_specs=pl.BlockSpec((1, 1, S, tq), lambda b, i: (b, i, 0, 0)),
        out_shape=jax.ShapeDtypeStruct((B, nq, S, tq), BF16),
        scratch_shapes=[pltpu.VMEM((IDX_DIM, IDX_HEADS * tq), BF16),
                        pltpu.VMEM((S, tq), jnp.int32),
                        pltpu.VMEM((1, tq), jnp.int32)],
        compiler_params=_params(("parallel", "arbitrary"), 40 << 20),
        name="dsa_index",
    )(zb, ki, wt)


def _dsa_attn_kernel(q_ref, k_ref, v_ref, bias_ref, o_ref, qt_ref, m_ref, l_ref, acc_ref, s_ref,
                     *, tq, tk, groups):
    i = pl.program_id(1)
    H = DSA_HEADS
    n_chunks = ((i + 1) * tq + tk - 1) // tk
    for h in range(H):
        qh = q_ref[:, h * HEAD_DIM:(h + 1) * HEAD_DIM].astype(F32)
        qt_ref[:, h * tq:(h + 1) * tq] = qh.T.astype(BF16)
    m_ref[...] = jnp.full(m_ref.shape, NEG, F32)
    l_ref[...] = jnp.zeros(l_ref.shape, F32)
    acc_ref[...] = jnp.zeros(acc_ref.shape, F32)

    gw = H * tq // groups

    n_total = k_ref.shape[0] // tk

    def stage(buf, c):
        start = pl.multiple_of(jnp.minimum(c, n_total - 1) * tk, tk)
        k = k_ref[pl.ds(start, tk), :]
        for g in range(groups):
            s_ref[buf, g] = jnp.dot(k, qt_ref[:, g * gw:(g + 1) * gw], preferred_element_type=F32)

    def consume(buf, c):
        start = pl.multiple_of(c * tk, tk)
        v = v_ref[pl.ds(start, tk), :]
        bias = bias_ref[0, 0, pl.ds(start, tk), :].astype(F32)
        bias = jnp.tile(bias, (1, gw // tq))
        for g in range(groups):
            sl = slice(g * gw, (g + 1) * gw)
            s = s_ref[buf, g] + bias
            m_prev = m_ref[:, sl]
            m_new = jnp.maximum(m_prev, jnp.max(s, axis=0, keepdims=True))
            alpha = jnp.exp2(m_prev - m_new)
            p = jnp.exp2(s - m_new)
            l_ref[:, sl] = alpha * l_ref[:, sl] + jnp.sum(p, axis=0, keepdims=True)
            acc_ref[:, sl] = alpha * acc_ref[:, sl] + lax.dot_general(
                v, p.astype(BF16), _TN, preferred_element_type=F32)
            m_ref[:, sl] = m_new

    if n_total % 2 == 0:
        stage(0, 0)

        def body(it, carry):
            stage(1, 2 * it + 1)
            consume(0, 2 * it)
            stage(0, 2 * it + 2)
            consume(1, 2 * it + 1)
            return carry

        lax.fori_loop(0, (n_chunks + 1) // 2, body, 0)
    else:
        def body(c, carry):
            stage(0, c)
            consume(0, c)
            return carry

        lax.fori_loop(0, n_chunks, body, 0)
    out = acc_ref[...] / l_ref[...]
    for h in range(H):
        o_ref[:, h * HEAD_DIM:(h + 1) * HEAD_DIM] = out[:, h * tq:(h + 1) * tq].T.astype(BF16)


def _dsa_attn(zb, kv, bias, B, S):
    M = B * S
    tq, tk = _tile(S, 128), _tile(S, 256)
    nq = S // tq
    groups = 4
    return pl.pallas_call(
        functools.partial(_dsa_attn_kernel, tq=tq, tk=tk, groups=groups),
        grid=(B, nq),
        in_specs=[pl.BlockSpec((tq, DSA_WIDTH), lambda b, i: (b * nq + i, BIG_QB // DSA_WIDTH)),
                  pl.BlockSpec((S, HEAD_DIM), lambda b, i: (b, 0)),
                  pl.BlockSpec((S, HEAD_DIM), lambda b, i: (b, 1)),
                  pl.BlockSpec((1, 1, S, tq), lambda b, i: (b, i, 0, 0))],
        out_specs=pl.BlockSpec((tq, DSA_WIDTH), lambda b, i: (b * nq + i, 0)),
        out_shape=jax.ShapeDtypeStruct((M, DSA_WIDTH), BF16),
        scratch_shapes=[pltpu.VMEM((HEAD_DIM, DSA_HEADS * tq), BF16),
                        pltpu.VMEM((1, DSA_HEADS * tq), F32),
                        pltpu.VMEM((1, DSA_HEADS * tq), F32),
                        pltpu.VMEM((HEAD_DIM, DSA_HEADS * tq), F32),
                        pltpu.VMEM((2, groups, tk, DSA_HEADS * tq // groups), F32)],
        compiler_params=_params(("parallel", "arbitrary"), 40 << 20),
        name="dsa_attn",
    )(zb, kv, kv, bias)


def _rope_tables(positions, dim):
    inv = ROPE_THETA ** (-jnp.arange(0, dim, 2, dtype=F32) / dim)
    ang = positions.astype(F32)[..., None] * inv
    cos, sin = jnp.cos(ang), jnp.sin(ang)
    reps = LANES // dim
    cosf = jnp.tile(jnp.concatenate([cos, cos], axis=-1), (1, 1, reps))
    sinf = jnp.tile(jnp.concatenate([-sin, sin], axis=-1), (1, 1, reps))
    return cosf.reshape(-1, LANES), sinf.reshape(-1, LANES)


def kernel(x, c, positions, w_ada, b_ada, norm_g, w_in, b_forget, w_proj_fox, w_proj_dsa,
           w_gate, w_out, w_ff_gate, w_ff_up, w_ff_down):
    B, S, D = x.shape
    M = B * S
    L = w_in.shape[0]
    F = w_ff_gate.shape[2]
    assert w_in.shape[2] == D_IN and S % CHUNK == 0
    top_k = min(TOPK_MAX, S // 4)
    idx_w_scale = IDX_WIDTH ** -0.5

    tm = _tile(M, 1024)
    tn_in = _tile(DSA_WIDTH, 1024)
    tn_merge = _tile(D, 256)
    tn_out = _tile(D, 1024)
    f_pad = -(-F // 1024) * 1024
    tn_ff = _tile(f_pad, 512)
    tk_down = f_pad // 4 if (f_pad // 4) % LANES == 0 else f_pad

    wb16 = lambda w: w.astype(BF16)
    w_big = wb16(jnp.concatenate([w_in[:, :, OFF_QA:OFF_FA], w_in[:, :, OFF_QB:OFF_KB],
                                  w_in[:, :, OFF_QI:OFF_KI]], axis=2))
    w_small = wb16(jnp.concatenate([w_in[:, :, OFF_KB:OFF_QI], w_in[:, :, OFF_KI:D_IN],
                                    w_in[:, :, OFF_FA:OFF_QB],
                                    jnp.zeros((L, D, SMALL_N - 2 * HEAD_DIM - MISC_FA - FOX_HEADS), F32)],
                                   axis=2))
    w_gate16, w_pf16, w_pd16, w_out16 = wb16(w_gate), wb16(w_proj_fox), wb16(w_proj_dsa), wb16(w_out)
    padf = lambda w, axis: jnp.pad(wb16(w), [(0, 0)] * axis + [(0, f_pad - F)] + [(0, 0)] * (2 - axis))
    w_fg16, w_fu16, w_fd16 = padf(w_ff_gate, 2), padf(w_ff_up, 2), padf(w_ff_down, 1)
    fbias = jnp.zeros((L, 1, LANES), F32).at[:, 0, MISC_FA:MISC_FA + FOX_HEADS].set(b_forget)

    cos_h, sin_h = _rope_tables(positions, HEAD_DIM)
    cos_i, sin_i = _rope_tables(positions, IDX_DIM)
    tab_spec = pl.BlockSpec((tm, LANES), lambda i, j: (i, 0))
    tables = [(t, tab_spec) for t in (cos_h, sin_h, cos_i, sin_i)]
    q_tables = [(t, tab_spec) for t in (cos_h * LOGIT_SCALE, sin_h * LOGIT_SCALE, cos_i, sin_i)]

    mods = _ada(c, w_ada, b_ada)

    _, h = _norm(x, None, mods[0], mods[0], norm_g[0], norm_g[0],
                 scale_row=1, shift_row=0, gb_row=0)
    for l in range(L):
        h2d = h.reshape(M, D)
        (zb,) = _matmul(
            [h2d], [(0, w_big, 0)], q_tables,
            [(jax.ShapeDtypeStruct((M, BIG_N), BF16), pl.BlockSpec((tm, tn_in), lambda i, j: (i, j)))],
            functools.partial(_epi_in_big, tn=tn_in), layer=l, tm=tm, tn=tn_in, n_tiles=BIG_N // tn_in,
            name="in_proj")
        kv, misc = _matmul(
            [h2d], [(0, w_small, 0)],
            tables + [(fbias[l], pl.BlockSpec((1, LANES), lambda i, j: (0, 0)))],
            [(jax.ShapeDtypeStruct((M, 2 * HEAD_DIM), BF16),
              pl.BlockSpec((tm, 2 * HEAD_DIM), lambda i, j: (i, 0))),
             (jax.ShapeDtypeStruct((M, LANES), F32), pl.BlockSpec((tm, LANES), lambda i, j: (i, 0)))],
            functools.partial(_epi_in_small, idx_w_scale=idx_w_scale), layer=l, tm=tm, tn=SMALL_N, n_tiles=1,
            name="in_proj_small")
        misc3 = misc.reshape(B, S, LANES)
        ki = misc3[:, :, MISC_KI:MISC_KI + IDX_DIM].astype(BF16)
        w_it = misc3[:, :, MISC_WI:MISC_WI + IDX_HEADS].transpose(0, 2, 1)
        log_f = misc3[:, :, MISC_FA:MISC_FA + FOX_HEADS].transpose(0, 2, 1)
        cum = _cumsum(log_f).transpose(0, 2, 1)

        y_a = _fox(zb, cum, B, S)
        bias = _dsa_index(zb, ki, w_it, B, S, top_k)
        y_b = _dsa_attn(zb, kv, bias, B, S)

        n_gate_tiles = D // tn_merge
        (mixed,) = _matmul(
            [h2d, y_a, y_b],
            [(0, w_gate16, 0), (0, w_gate16, n_gate_tiles), (1, w_pf16, 0), (2, w_pd16, 0)],
            [], [(jax.ShapeDtypeStruct((M, D), BF16), pl.BlockSpec((tm, tn_merge), lambda i, j: (i, j)))],
            _epi_merge, layer=l, tm=tm, tn=tn_merge, n_tiles=n_gate_tiles, name="gate_merge")
        (yo,) = _matmul(
            [mixed], [(0, w_out16, 0)], [],
            [(jax.ShapeDtypeStruct((M, D), F32), pl.BlockSpec((tm, tn_out), lambda i, j: (i, j)))],
            _epi_plain, layer=l, tm=tm, tn=tn_out, n_tiles=D // tn_out, name="out_proj")
        x, h = _norm(x, yo.reshape(B, S, D), mods[l], mods[l], norm_g[l], norm_g[l],
                     gate_row=2, ga_row=1, scale_row=4, shift_row=3, gb_row=2)

        (t,) = _matmul(
            [h.reshape(M, D)], [(0, w_fg16, 0), (0, w_fu16, 0)], [],
            [(jax.ShapeDtypeStruct((M, f_pad), BF16), pl.BlockSpec((tm, tn_ff), lambda i, j: (i, j)))],
            _epi_swiglu, layer=l, tm=tm, tn=tn_ff, n_tiles=f_pad // tn_ff, name="ffn_gate_up")
        ff = _matmul_ktiled(t, w_fd16, layer=l, tm=tm, tn=_tile(D, 1024), tk=tk_down)
        last = l == L - 1
        nl = l if last else l + 1
        x, h = _norm(x, ff.reshape(B, S, D), mods[l], mods[nl], norm_g[l], norm_g[nl],
                     gate_row=5, ga_row=3, scale_row=1, shift_row=0, gb_row=0, emit_h=not last)
    return x
```

```python
import functools

import jax
import jax.numpy as jnp
from jax import lax
from jax.experimental import pallas as pl
from jax.experimental.pallas import tpu as pltpu

F32 = jnp.float32
BF16 = jnp.bfloat16

CHUNK = 64
HEAD_DIM = 128
FOX_HEADS = 16
DSA_HEADS = 16
FOX_WIDTH = FOX_HEADS * HEAD_DIM
DSA_WIDTH = DSA_HEADS * HEAD_DIM
IDX_HEADS = 32
IDX_DIM = 64
IDX_WIDTH = IDX_HEADS * IDX_DIM
TOPK_MAX = 256
ROPE_THETA = 10000.0
EPS = 1e-6
N_MOD = 6

OFF_QA = 0
OFF_FA = 3 * FOX_WIDTH
OFF_QB = OFF_FA + FOX_HEADS
OFF_KB = OFF_QB + DSA_WIDTH
OFF_QI = OFF_KB + 2 * HEAD_DIM
OFF_KI = OFF_QI + IDX_WIDTH
OFF_WI = OFF_KI + IDX_DIM
D_IN = OFF_WI + IDX_HEADS

BIG_QB = 3 * FOX_WIDTH
BIG_QI = BIG_QB + DSA_WIDTH
BIG_N = BIG_QI + IDX_WIDTH
MISC_KI = 0
MISC_WI = IDX_DIM
MISC_FA = IDX_DIM + IDX_HEADS
SMALL_N = 3 * HEAD_DIM

LANES = 128
V7X_VMEM_BYTES = 64 * 1024 * 1024
VMEM_BUDGET = V7X_VMEM_BYTES - 8 * 1024 * 1024

NEG = -1e30
INT_MIN = -(2 ** 31)
LOG2E = 1.4426950408889634
LOGIT_SCALE = HEAD_DIM ** -0.5 * LOG2E


def _params(semantics, vmem_bytes):
    return pltpu.CompilerParams(dimension_semantics=semantics,
                                vmem_limit_bytes=int(min(vmem_bytes, VMEM_BUDGET)))


def _tile(n, pref):
    if n <= pref:
        return n
    t = (pref // LANES) * LANES
    while t > LANES and n % t:
        t -= LANES
    assert n % t == 0, (n, pref)
    return t


def _ada_kernel(c_ref, w_ref, b_ref, o_ref):
    c = c_ref[...]
    s = c * jax.nn.sigmoid(c)
    acc = jnp.dot(s.astype(BF16), w_ref[...].astype(BF16), preferred_element_type=F32)
    o_ref[...] = acc[None] + b_ref[...]


def _ada(c, w_ada, b_ada):
    B, D = c.shape
    L = b_ada.shape[0]
    N = w_ada.shape[1]
    rows = 8
    cp = jnp.zeros((rows, D), F32).at[:B].set(c)
    tn = _tile(N, 512)
    out = pl.pallas_call(
        _ada_kernel,
        grid=(N // tn,),
        in_specs=[pl.BlockSpec((rows, D), lambda j: (0, 0)),
                  pl.BlockSpec((D, tn), lambda j: (0, j)),
                  pl.BlockSpec((L, 1, tn), lambda j: (0, 0, j))],
        out_specs=pl.BlockSpec((L, rows, tn), lambda j: (0, 0, j)),
        out_shape=jax.ShapeDtypeStruct((L, rows, N), F32),
        compiler_params=_params(("parallel",), 4 * D * tn * 4 + (8 << 20)),
        name="ada",
    )(cp, w_ada, b_ada.reshape(L, 1, N))
    return out[:, :B].reshape(L, B, N_MOD, D)


def _rms(v, g):
    return v * lax.rsqrt(jnp.mean(v * v, axis=-1, keepdims=True) + EPS) * g


def _norm_kernel(*refs, resid, emit_h, gate_row, ga_row, scale_row, shift_row, gb_row):
    refs = list(refs)
    x_ref = refs.pop(0)
    y_ref = refs.pop(0) if resid else None
    moda_ref, modb_ref, ga_ref, gb_ref = refs[:4]
    outs = refs[4:]
    x = x_ref[0]
    if resid:
        gate = moda_ref[0, gate_row:gate_row + 1, :]
        x = x + gate * _rms(y_ref[0], ga_ref[ga_row:ga_row + 1, :])
        outs.pop(0)[0] = x
    if emit_h:
        scale = modb_ref[0, scale_row:scale_row + 1, :]
        shift = modb_ref[0, shift_row:shift_row + 1, :]
        h = _rms(x, gb_ref[gb_row:gb_row + 1, :]) * (1.0 + scale) + shift
        outs.pop(0)[0] = h.astype(BF16)


def _norm(x, y, moda, modb, ga, gb, *, gate_row=0, ga_row=0, scale_row=0, shift_row=0,
          gb_row=0, emit_h=True):
    B, S, D = x.shape
    resid = y is not None
    ts = _tile(S, 256)
    row = pl.BlockSpec((1, ts, D), lambda b, i: (b, i, 0))
    mod = pl.BlockSpec((1, N_MOD, D), lambda b, i: (b, 0, 0))
    gsp = pl.BlockSpec((4, D), lambda b, i: (0, 0))
    in_specs = [row] + ([row] if resid else []) + [mod, mod, gsp, gsp]
    args = [x] + ([y] if resid else []) + [moda, modb, ga, gb]
    out_shape, out_specs = [], []
    if resid:
        out_shape.append(jax.ShapeDtypeStruct((B, S, D), F32))
        out_specs.append(row)
    if emit_h:
        out_shape.append(jax.ShapeDtypeStruct((B, S, D), BF16))
        out_specs.append(row)
    n_rows_f32 = 1 + 2 * resid + 0.5 * emit_h
    outs = pl.pallas_call(
        functools.partial(_norm_kernel, resid=resid, emit_h=emit_h, gate_row=gate_row,
                          ga_row=ga_row, scale_row=scale_row, shift_row=shift_row,
                          gb_row=gb_row),
        grid=(B, S // ts),
        in_specs=in_specs, out_specs=out_specs, out_shape=out_shape,
        compiler_params=_params(("parallel", "parallel"),
                                2 * n_rows_f32 * ts * D * 4 + 6 * ts * D * 4 + (4 << 20)),
        name="norm",
    )(*args)
    outs = list(outs)
    x_new = outs.pop(0) if resid else None
    h = outs.pop(0) if emit_h else None
    return x_new, h


def _mm_kernel(*refs, n_a, pairs, n_extra, epilogue):
    n_w = len(pairs)
    a_refs = refs[:n_a]
    w_refs = refs[n_a:n_a + n_w]
    extra = refs[n_a + n_w:n_a + n_w + n_extra]
    outs = refs[n_a + n_w + n_extra:]
    accs = [jnp.dot(a_refs[ai][...], w_refs[wi][...], preferred_element_type=F32)
            for wi, ai in enumerate(pairs)]
    epilogue(accs, extra, outs)


def _matmul(a_list, w_list, extras, outs, epilogue, *, layer, tm, tn, n_tiles, name):
    M = a_list[0].shape[0]
    in_specs = [pl.BlockSpec((tm, a.shape[1]), lambda i, j: (i, 0)) for a in a_list]
    vmem = sum(2 * tm * a.shape[1] * a.dtype.itemsize for a in a_list)
    for _, w, off in w_list:
        in_specs.append(pl.BlockSpec((None, w.shape[1], tn),
                                     lambda i, j, off=off: (layer, 0, j + off)))
        vmem += 2 * w.shape[1] * tn * w.dtype.itemsize + 2 * tm * tn * 4
    in_specs += [sp for _, sp in extras]
    vmem += sum(2 * tm * LANES * 4 for _ in extras)
    vmem += sum(2 * tm * sp.block_shape[-1] * sd.dtype.itemsize for sd, sp in outs)
    return pl.pallas_call(
        functools.partial(_mm_kernel, n_a=len(a_list), pairs=tuple(ai for ai, _, _ in w_list),
                          n_extra=len(extras), epilogue=epilogue),
        grid=(M // tm, n_tiles),
        in_specs=in_specs,
        out_specs=[sp for _, sp in outs],
        out_shape=[sd for sd, _ in outs],
        compiler_params=_params(("parallel", "arbitrary"), vmem + (4 << 20)),
        name=name,
    )(*a_list, *[w for _, w, _ in w_list], *[e for e, _ in extras])


def _rope_slab(a, cosf, sinf, half):
    if 2 * half == LANES:
        partner = pltpu.roll(a, half, 1)
    else:
        lane = lax.broadcasted_iota(jnp.int32, a.shape, 1)
        partner = jnp.where(lane % (2 * half) < half,
                            pltpu.roll(a, LANES - half, 1), pltpu.roll(a, half, 1))
    return a * cosf + partner * sinf


def _epi_in_big(accs, extra, outs, *, tn):
    cos_h, sin_h, cos_i, sin_i = extra
    (o_ref,) = outs
    acc = accs[0]
    j = pl.program_id(1)
    t_ka, t_qb, t_qi = FOX_WIDTH // tn, BIG_QB // tn, BIG_QI // tn

    @pl.when(j < t_ka)
    def _():
        o_ref[...] = (acc * LOGIT_SCALE).astype(BF16)

    @pl.when((j >= t_ka) & (j < t_qb))
    def _():
        o_ref[...] = acc.astype(BF16)

    def roped(cos_ref, sin_ref, half):
        cosf, sinf = cos_ref[...], sin_ref[...]
        for s in range(tn // LANES):
            sl = slice(s * LANES, (s + 1) * LANES)
            o_ref[:, sl] = _rope_slab(acc[:, sl], cosf, sinf, half).astype(BF16)

    @pl.when((j >= t_qb) & (j < t_qi))
    def _():
        roped(cos_h, sin_h, HEAD_DIM // 2)

    @pl.when(j >= t_qi)
    def _():
        roped(cos_i, sin_i, IDX_DIM // 2)


def _epi_in_small(accs, extra, outs, *, idx_w_scale):
    cos_h, sin_h, cos_i, sin_i, fbias = extra
    kv_ref, misc_ref = outs
    acc = accs[0]
    kb = _rope_slab(acc[:, 0:LANES], cos_h[...], sin_h[...], HEAD_DIM // 2)
    kv_ref[:, 0:LANES] = kb.astype(BF16)
    kv_ref[:, LANES:2 * LANES] = acc[:, LANES:2 * LANES].astype(BF16)
    a = acc[:, 2 * LANES:3 * LANES]
    ki = _rope_slab(a, cos_i[...], sin_i[...], IDX_DIM // 2)
    f = a + fbias[...]
    log_f = jnp.minimum(f, 0.0) - jnp.log1p(jnp.exp(-jnp.abs(f)))
    lane = lax.broadcasted_iota(jnp.int32, a.shape, 1)
    misc_ref[...] = jnp.where(lane < MISC_WI, ki,
                              jnp.where(lane < MISC_FA, a * idx_w_scale, log_f))


def _epi_merge(accs, extra, outs):
    ga, gb, pa, pb = accs
    outs[0][...] = (jax.nn.sigmoid(ga) * pa + jax.nn.sigmoid(gb) * pb).astype(BF16)


def _epi_plain(accs, extra, outs):
    outs[0][...] = accs[0].astype(outs[0].dtype)


def _epi_swiglu(accs, extra, outs):
    g, u = accs
    outs[0][...] = (g * jax.nn.sigmoid(g) * u).astype(BF16)


def _cumsum_kernel(x_ref, o_ref, *, width):
    S = x_ref.shape[2]
    r = lax.broadcasted_iota(jnp.int32, (width, width), 0)
    c = lax.broadcasted_iota(jnp.int32, (width, width), 1)
    upper = jnp.where(r <= c, 1.0, 0.0).astype(BF16)
    carry = jnp.zeros((x_ref.shape[1], 1), F32)
    for i in range(S // width):
        x = x_ref[0, :, i * width:(i + 1) * width]
        hi = x.astype(BF16)
        r1 = x - hi.astype(F32)
        mid = r1.astype(BF16)
        lo = (r1 - mid.astype(F32)).astype(BF16)
        y = (jnp.dot(hi, upper, preferred_element_type=F32)
             + jnp.dot(mid, upper, preferred_element_type=F32)
             + jnp.dot(lo, upper, preferred_element_type=F32)) + carry
        o_ref[0, :, i * width:(i + 1) * width] = y
        carry = y[:, width - 1:width]


def _cumsum(x):
    B, H, S = x.shape
    width = _tile(S, 256)
    spec = pl.BlockSpec((1, H, S), lambda b: (b, 0, 0))
    return pl.pallas_call(
        functools.partial(_cumsum_kernel, width=width),
        grid=(B,), in_specs=[spec], out_specs=spec,
        out_shape=jax.ShapeDtypeStruct((B, H, S), F32),
        compiler_params=_params(("parallel",), 16 << 20),
        name="cumsum",
    )(x)


_NT = (((1,), (1,)), ((), ()))
_TN = (((0,), (0,)), ((), ()))


def _fox_kernel(q_ref, k_ref, v_ref, cum_ref, o_ref, ck_ref, m_ref, l_ref, acc_ref, s0_ref,
                *, tq, hg):
    g = pl.program_id(1)
    i = pl.program_id(2)
    S = k_ref.shape[0]
    D = HEAD_DIM

    @pl.when(i == 0)
    def _():
        lane = lax.broadcasted_iota(jnp.int32, (1, FOX_HEADS), 1)
        for c in range(S // tq):
            blk = cum_ref[0, c * tq:(c + 1) * tq, :]
            for j in range(hg):
                onehot = jnp.where(lane == g * hg + j, LOG2E, 0.0)
                col = jnp.sum(blk * onehot, axis=1, keepdims=True)
                ck_ref[j, c * tq:(c + 1) * tq, :] = jnp.broadcast_to(col, (tq, LANES))

    m_ref[...] = jnp.full(m_ref.shape, NEG, F32)
    l_ref[...] = jnp.zeros(l_ref.shape, F32)
    acc_ref[...] = jnp.zeros(acc_ref.shape, F32)

    def logits(c, j):
        start = pl.multiple_of(c * tq, tq)
        return lax.dot_general(k_ref[pl.ds(start, tq), j * D:(j + 1) * D],
                               q_ref[:, j * D:(j + 1) * D], _NT, preferred_element_type=F32)

    s0_ref[...] = logits(0, 0)

    def chunk(c, diagonal):
        start = pl.multiple_of(c * tq, tq)
        s_next = s0_ref[...]
        for j in range(hg):
            s = s_next
            if j + 1 < hg:
                s_next = logits(c, j + 1)
            elif not diagonal:
                s0_ref[...] = logits(c + 1, 0)
            v = v_ref[pl.ds(start, tq), j * D:(j + 1) * D]
            s = s - jnp.tile(ck_ref[j, pl.ds(start, tq), :], (1, tq // LANES))
            if diagonal:
                key = lax.broadcasted_iota(jnp.int32, s.shape, 0)
                qry = lax.broadcasted_iota(jnp.int32, s.shape, 1)
                s = jnp.where(key <= qry, s, NEG)
            m_prev = m_ref[j]
            m_new = jnp.maximum(m_prev, jnp.max(s, axis=0, keepdims=True))
            alpha = jnp.exp2(m_prev - m_new)
            p = jnp.exp2(s - m_new)
            l_ref[j] = alpha * l_ref[j] + jnp.sum(p, axis=0, keepdims=True)
            acc_ref[j] = alpha * acc_ref[j] + lax.dot_general(
                v, p.astype(BF16), _TN, preferred_element_type=F32)
            m_ref[j] = m_new

    def body(c, carry):
        chunk(c, False)
        return carry

    lax.fori_loop(0, i, body, 0)
    chunk(i, True)
    for j in range(hg):
        o_ref[:, j * D:(j + 1) * D] = (acc_ref[j] / l_ref[j]).T.astype(BF16)


def _fox(zb, cum, B, S):
    M = B * S
    tq = _tile(S, 512)
    nq = S // tq
    H = FOX_HEADS
    hg = 4
    G = H // hg
    W = hg * HEAD_DIM
    return pl.pallas_call(
        functools.partial(_fox_kernel, tq=tq, hg=hg),
        grid=(B, G, nq),
        in_specs=[pl.BlockSpec((tq, W), lambda b, g, i: (b * nq + i, g)),
                  pl.BlockSpec((S, W), lambda b, g, i: (b, G + g)),
                  pl.BlockSpec((S, W), lambda b, g, i: (b, 2 * G + g)),
                  pl.BlockSpec((1, S, H), lambda b, g, i: (b, 0, 0))],
        out_specs=pl.BlockSpec((tq, W), lambda b, g, i: (b * nq + i, g)),
        out_shape=jax.ShapeDtypeStruct((M, FOX_WIDTH), BF16),
        scratch_shapes=[pltpu.VMEM((hg, S, LANES), F32),
                        pltpu.VMEM((hg, 1, tq), F32), pltpu.VMEM((hg, 1, tq), F32),
                        pltpu.VMEM((hg, HEAD_DIM, tq), F32),
                        pltpu.VMEM((tq, tq), F32)],
        compiler_params=_params(("parallel", "parallel", "arbitrary"), 40 << 20),
        name="fox_attn",
    )(zb, zb, zb, cum)


def _dsa_index_kernel(qi_ref, ki_ref, wt_ref, bias_ref, qt_ref, key_ref, x_ref,
                      *, tq, tc, top_k, n_idx_bits):
    i = pl.program_id(1)
    S = ki_ref.shape[1]
    n_chunks = ((i + 1) * tq + tc - 1) // tc

    for p in range(IDX_WIDTH // LANES):
        blk = qi_ref[:, p * LANES:(p + 1) * LANES].astype(F32).T
        qt_ref[:, (2 * p) * tq:(2 * p + 1) * tq] = blk[:IDX_DIM].astype(BF16)
        qt_ref[:, (2 * p + 1) * tq:(2 * p + 2) * tq] = blk[IDX_DIM:].astype(BF16)
    bias_ref[...] = jnp.full(bias_ref.shape, NEG, BF16)

    t = i * tq + lax.broadcasted_iota(jnp.int32, (1, tq), 1)
    lim = (t // CHUNK + 1) * CHUNK

    def row_of(c):
        return c * tc + lax.broadcasted_iota(jnp.int32, (tc, tq), 0)

    def score_chunk(c, carry):
        start = pl.multiple_of(c * tc, tc)
        rel = jnp.dot(ki_ref[0, pl.ds(start, tc), :], qt_ref[...],
                      preferred_element_type=F32)
        acc = jnp.zeros((tc, tq), F32)
        for h in range(IDX_HEADS):
            acc = acc + wt_ref[0, h:h + 1, :] * jnp.maximum(rel[:, h * tq:(h + 1) * tq], 0.0)
        bits = pltpu.bitcast(acc, jnp.int32)
        key = bits ^ ((bits >> 31) & 0x7FFFFFFF)
        key = jnp.where(acc == 0.0, 0, key)
        key_ref[pl.ds(start, tc), :] = jnp.where(row_of(c) < lim, key, INT_MIN)
        return carry

    lax.fori_loop(0, n_chunks, score_chunk, 0)

    pair = 2 if (S // tc) % 2 == 0 else 1
    n_cnt = (n_chunks + pair - 1) // pair
    tcc = pair * tc
    if pair == 2:
        @pl.when(n_chunks % 2 == 1)
        def _():
            key_ref[pl.ds(pl.multiple_of(n_chunks * tc, tc), tc), :] = jnp.full(
                (tc, tq), INT_MIN, jnp.int32)

    def count(*preds):
        def body(c, cnts):
            kc = key_ref[pl.ds(pl.multiple_of(c * tcc, tcc), tcc), :]
            return tuple(cnt + jnp.sum(jnp.where(pred(kc, c), 1, 0).reshape(tcc // 32, 32, tq), axis=0)
                         for cnt, pred in zip(cnts, preds))
        cnts = lax.fori_loop(0, n_cnt, body,
                             tuple(jnp.zeros((32, tq), jnp.int32) for _ in preds))
        return [jnp.sum(cnt, axis=0, keepdims=True) for cnt in cnts]

    def cnt_row(c):
        return c * tcc + lax.broadcasted_iota(jnp.int32, (tcc, tq), 0)

    def bit_body(it, T):
        cand = T + lax.shift_left(jnp.int32(1), 31 - it)
        (n_ge,) = count(lambda kc, c: kc >= cand)
        return jnp.where(n_ge >= top_k, cand, T)

    T = lax.fori_loop(0, 32, bit_body, jnp.full((1, tq), INT_MIN, jnp.int32))
    n_ge, n_gt = count(lambda kc, c: kc >= T, lambda kc, c: kc > T)
    need = top_k - n_gt
    excess = jnp.where((n_ge > top_k) & (T != INT_MIN), 1, 0)
    x_ref[...] = jnp.full((1, tq), S, jnp.int32)

    @pl.when(jnp.max(excess) > 0)
    def _():
        def x_body(it, X):
            cand = X + lax.shift_left(jnp.int32(1), n_idx_bits - 1 - it)
            (n,) = count(lambda kc, c: (kc == T) & (cnt_row(c) < cand))
            return jnp.where(n <= need - 1, cand, X)
        x_ref[...] = lax.fori_loop(0, n_idx_bits, x_body, jnp.zeros((1, tq), jnp.int32))

    X = x_ref[...]

    def write_chunk(c, carry):
        start = pl.multiple_of(c * tc, tc)
        kc = key_ref[pl.ds(start, tc), :]
        row = row_of(c)
        tie = jnp.where(kc == T, jnp.where(row <= X, 0.0, NEG), NEG)
        sel = jnp.where(kc > T, 0.0, tie)
        bias_ref[0, 0, pl.ds(start, tc), :] = jnp.where(row < lim, sel, NEG).astype(BF16)
        return carry

    lax.fori_loop(0, n_chunks, write_chunk, 0)


def _dsa_index(zb, ki, wt, B, S, top_k):
    tq, tc = _tile(S, 128), _tile(S, 256)
    nq = S // tq
    n_idx_bits = max(1, (S - 1).bit_length())
    return pl.pallas_call(
        functools.partial(_dsa_index_kernel, tq=tq, tc=tc, top_k=top_k, n_idx_bits=n_idx_bits),
        grid=(B, nq),
        in_specs=[pl.BlockSpec((tq, IDX_WIDTH), lambda b, i: (b * nq + i, BIG_QI // IDX_WIDTH)),
                  pl.BlockSpec((1, S, IDX_DIM), lambda b, i: (b, 0, 0)),
                  pl.BlockSpec((1, IDX_HEADS, tq), lambda b, i: (b, 0, i))],
        out_specs=pl.BlockSpec((1, 1, S, tq), lambda b, i: (b, i, 0, 0)),
        out_shape=jax.ShapeDtypeStruct((B, nq, S, tq), BF16),
        scratch_shapes=[pltpu.VMEM((IDX_DIM, IDX_HEADS * tq), BF16),
                        pltpu.VMEM((S, tq), jnp.int32),
                        pltpu.VMEM((1, tq), jnp.int32)],
        compiler_params=_params(("parallel", "arbitrary"), 40 << 20),
        name="dsa_index",
    )(zb, ki, wt)


def _dsa_attn_kernel(q_ref, k_ref, v_ref, bias_ref, o_ref, qt_ref, m_ref, l_ref, acc_ref, s_ref,
                     *, tq, tk, groups):
    i = pl.program_id(1)
    H = DSA_HEADS
    n_chunks = ((i + 1) * tq + tk - 1) // tk
    for h in range(H):
        qh = q_ref[:, h * HEAD_DIM:(h + 1) * HEAD_DIM].astype(F32)
        qt_ref[:, h * tq:(h + 1) * tq] = qh.T.astype(BF16)
    m_ref[...] = jnp.full(m_ref.shape, NEG, F32)
    l_ref[...] = jnp.zeros(l_ref.shape, F32)
    acc_ref[...] = jnp.zeros(acc_ref.shape, F32)

    gw = H * tq // groups

    n_total = k_ref.shape[0] // tk

    def stage(buf, c):
        start = pl.multiple_of(jnp.minimum(c, n_total - 1) * tk, tk)
        k = k_ref[pl.ds(start, tk), :]
        for g in range(groups):
            s_ref[buf, g] = jnp.dot(k, qt_ref[:, g * gw:(g + 1) * gw], preferred_element_type=F32)

    def consume(buf, c):
        start = pl.multiple_of(c * tk, tk)
        v = v_ref[pl.ds(start, tk), :]
        bias = bias_ref[0, 0, pl.ds(start, tk), :].astype(F32)
        bias = jnp.tile(bias, (1, gw // tq))
        for g in range(groups):
            sl = slice(g * gw, (g + 1) * gw)
            s = s_ref[buf, g] + bias
            m_prev = m_ref[:, sl]
            m_new = jnp.maximum(m_prev, jnp.max(s, axis=0, keepdims=True))
            alpha = jnp.exp2(m_prev - m_new)
            p = jnp.exp2(s - m_new)
            l_ref[:, sl] = alpha * l_ref[:, sl] + jnp.sum(p, axis=0, keepdims=True)
            acc_ref[:, sl] = alpha * acc_ref[:, sl] + lax.dot_general(
                v, p.astype(BF16), _TN, preferred_element_type=F32)
            m_ref[:, sl] = m_new

    if n_total % 2 == 0:
        stage(0, 0)

        def body(it, carry):
            stage(1, 2 * it + 1)
            consume(0, 2 * it)
            stage(0, 2 * it + 2)
            consume(1, 2 * it + 1)
            return carry

        lax.fori_loop(0, (n_chunks + 1) // 2, body, 0)
    else:
        def body(c, carry):
            stage(0, c)
            consume(0, c)
            return carry

        lax.fori_loop(0, n_chunks, body, 0)
    out = acc_ref[...] / l_ref[...]
    for h in range(H):
        o_ref[:, h * HEAD_DIM:(h + 1) * HEAD_DIM] = out[:, h * tq:(h + 1) * tq].T.astype(BF16)


def _dsa_attn(zb, kv, bias, B, S):
    M = B * S
    tq, tk = _tile(S, 128), _tile(S, 256)
    nq = S // tq
    groups = 4
    return pl.pallas_call(
        functools.partial(_dsa_attn_kernel, tq=tq, tk=tk, groups=groups),
        grid=(B, nq),
        in_specs=[pl.BlockSpec((tq, DSA_WIDTH), lambda b, i: (b * nq + i, BIG_QB // DSA_WIDTH)),
                  pl.BlockSpec((S, HEAD_DIM), lambda b, i: (b, 0)),
                  pl.BlockSpec((S, HEAD_DIM), lambda b, i: (b, 1)),
                  pl.BlockSpec((1, 1, S, tq), lambda b, i: (b, i, 0, 0))],
        out_specs=pl.BlockSpec((tq, DSA_WIDTH), lambda b, i: (b * nq + i, 0)),
        out_shape=jax.ShapeDtypeStruct((M, DSA_WIDTH), BF16),
        scratch_shapes=[pltpu.VMEM((HEAD_DIM, DSA_HEADS * tq), BF16),
                        pltpu.VMEM((1, DSA_HEADS * tq), F32),
                        pltpu.VMEM((1, DSA_HEADS * tq), F32),
                        pltpu.VMEM((HEAD_DIM, DSA_HEADS * tq), F32),
                        pltpu.VMEM((2, groups, tk, DSA_HEADS * tq // groups), F32)],
        compiler_params=_params(("parallel", "arbitrary"), 40 << 20),
        name="dsa_attn",
    )(zb, kv, kv, bias)


def _rope_tables(positions, dim):
    inv = ROPE_THETA ** (-jnp.arange(0, dim, 2, dtype=F32) / dim)
    ang = positions.astype(F32)[..., None] * inv
    cos, sin = jnp.cos(ang), jnp.sin(ang)
    reps = LANES // dim
    cosf = jnp.tile(jnp.concatenate([cos, cos], axis=-1), (1, 1, reps))
    sinf = jnp.tile(jnp.concatenate([-sin, sin], axis=-1), (1, 1, reps))
    return cosf.reshape(-1, LANES), sinf.reshape(-1, LANES)


def kernel(x, c, positions, w_ada, b_ada, norm_g, w_in, b_forget, w_proj_fox, w_proj_dsa,
           w_gate, w_out, w_ff_gate, w_ff_up, w_ff_down):
    B, S, D = x.shape
    M = B * S
    L = w_in.shape[0]
    F = w_ff_gate.shape[2]
    assert w_in.shape[2] == D_IN and S % CHUNK == 0
    top_k = min(TOPK_MAX, S // 4)
    idx_w_scale = IDX_WIDTH ** -0.5

    tm = _tile(M, 1024)
    tn_in = _tile(DSA_WIDTH, 1024)
    tn_merge = _tile(D, 256)
    tn_out = _tile(D, 1024)
    tn_ff = _tile(F, 512)
    tm_ff = _tile(M, 2048 if tn_ff <= 256 else 1024)
    tm_down, tn_down = _tile(M, 512), _tile(D, 512)

    wb16 = lambda w: w.astype(BF16)
    w_big = jnp.concatenate([wb16(w_in[:, :, OFF_QA:OFF_FA]), wb16(w_in[:, :, OFF_QB:OFF_KB]),
                             wb16(w_in[:, :, OFF_QI:OFF_KI])], axis=2)
    w_small = jnp.concatenate([wb16(w_in[:, :, OFF_KB:OFF_QI]), wb16(w_in[:, :, OFF_KI:D_IN]),
                               wb16(w_in[:, :, OFF_FA:OFF_QB]),
                               jnp.zeros((L, D, SMALL_N - 2 * HEAD_DIM - MISC_FA - FOX_HEADS), BF16)],
                              axis=2)
    w_gate16, w_pf16, w_pd16, w_out16 = wb16(w_gate), wb16(w_proj_fox), wb16(w_proj_dsa), wb16(w_out)
    w_fg16, w_fu16, w_fd16 = wb16(w_ff_gate), wb16(w_ff_up), wb16(w_ff_down)
    fbias = jnp.zeros((L, 1, LANES), F32).at[:, 0, MISC_FA:MISC_FA + FOX_HEADS].set(b_forget)

    cos_h, sin_h = _rope_tables(positions, HEAD_DIM)
    cos_i, sin_i = _rope_tables(positions, IDX_DIM)
    tab_spec = pl.BlockSpec((tm, LANES), lambda i, j: (i, 0))
    tables = [(t, tab_spec) for t in (cos_h, sin_h, cos_i, sin_i)]
    q_tables = [(t, tab_spec) for t in (cos_h * LOGIT_SCALE, sin_h * LOGIT_SCALE, cos_i, sin_i)]

    mods = _ada(c, w_ada, b_ada)

    _, h = _norm(x, None, mods[0], mods[0], norm_g[0], norm_g[0],
                 scale_row=1, shift_row=0, gb_row=0)
    for l in range(L):
        h2d = h.reshape(M, D)
        (zb,) = _matmul(
            [h2d], [(0, w_big, 0)], q_tables,
            [(jax.ShapeDtypeStruct((M, BIG_N), BF16), pl.BlockSpec((tm, tn_in), lambda i, j: (i, j)))],
            functools.partial(_epi_in_big, tn=tn_in), layer=l, tm=tm, tn=tn_in, n_tiles=BIG_N // tn_in,
            name="in_proj")
        kv, misc = _matmul(
            [h2d], [(0, w_small, 0)],
            tables + [(fbias[l], pl.BlockSpec((1, LANES), lambda i, j: (0, 0)))],
            [(jax.ShapeDtypeStruct((M, 2 * HEAD_DIM), BF16),
              pl.BlockSpec((tm, 2 * HEAD_DIM), lambda i, j: (i, 0))),
             (jax.ShapeDtypeStruct((M, LANES), F32), pl.BlockSpec((tm, LANES), lambda i, j: (i, 0)))],
            functools.partial(_epi_in_small, idx_w_scale=idx_w_scale), layer=l, tm=tm, tn=SMALL_N, n_tiles=1,
            name="in_proj_small")
        misc3 = misc.reshape(B, S, LANES)
        ki = misc3[:, :, MISC_KI:MISC_KI + IDX_DIM].astype(BF16)
        w_it = misc3[:, :, MISC_WI:MISC_WI + IDX_HEADS].transpose(0, 2, 1)
        log_f = misc3[:, :, MISC_FA:MISC_FA + FOX_HEADS].transpose(0, 2, 1)
        cum = _cumsum(log_f).transpose(0, 2, 1)

        y_a = _fox(zb, cum, B, S)
        bias = _dsa_index(zb, ki, w_it, B, S, top_k)
        y_b = _dsa_attn(zb, kv, bias, B, S)

        n_gate_tiles = D // tn_merge
        (mixed,) = _matmul(
            [h2d, y_a, y_b],
            [(0, w_gate16, 0), (0, w_gate16, n_gate_tiles), (1, w_pf16, 0), (2, w_pd16, 0)],
            [], [(jax.ShapeDtypeStruct((M, D), BF16), pl.BlockSpec((tm, tn_merge), lambda i, j: (i, j)))],
            _epi_merge, layer=l, tm=tm, tn=tn_merge, n_tiles=n_gate_tiles, name="gate_merge")
        (yo,) = _matmul(
            [mixed], [(0, w_out16, 0)], [],
            [(jax.ShapeDtypeStruct((M, D), F32), pl.BlockSpec((tm, tn_out), lambda i, j: (i, j)))],
            _epi_plain, layer=l, tm=tm, tn=tn_out, n_tiles=D // tn_out, name="out_proj")
        x, h = _norm(x, yo.reshape(B, S, D), mods[l], mods[l], norm_g[l], norm_g[l],
                     gate_row=2, ga_row=1, scale_row=4, shift_row=3, gb_row=2)

        (t,) = _matmul(
            [h.reshape(M, D)], [(0, w_fg16, 0), (0, w_fu16, 0)], [],
            [(jax.ShapeDtypeStruct((M, F), BF16), pl.BlockSpec((tm_ff, tn_ff), lambda i, j: (i, j)))],
            _epi_swiglu, layer=l, tm=tm_ff, tn=tn_ff, n_tiles=F // tn_ff, name="ffn_gate_up")
        (ff,) = _matmul(
            [t], [(0, w_fd16, 0)], [],
            [(jax.ShapeDtypeStruct((M, D), F32), pl.BlockSpec((tm_down, tn_down), lambda i, j: (i, j)))],
            _epi_plain, layer=l, tm=tm_down, tn=tn_down, n_tiles=D // tn_down, name="ffn_down")
        last = l == L - 1
        nl = l if last else l + 1
        x, h = _norm(x, ff.reshape(B, S, D), mods[l], mods[nl], norm_g[l], norm_g[nl],
                     gate_row=5, ga_row=3, scale_row=1, shift_row=0, gb_row=0, emit_h=not last)
    return x
```

```python
import functools

import jax
import jax.numpy as jnp
from jax import lax
from jax.experimental import pallas as pl
from jax.experimental.pallas import tpu as pltpu

F32 = jnp.float32
BF16 = jnp.bfloat16

CHUNK = 64
HEAD_DIM = 128
FOX_HEADS = 16
DSA_HEADS = 16
FOX_WIDTH = FOX_HEADS * HEAD_DIM
DSA_WIDTH = DSA_HEADS * HEAD_DIM
IDX_HEADS = 32
IDX_DIM = 64
IDX_WIDTH = IDX_HEADS * IDX_DIM
TOPK_MAX = 256
ROPE_THETA = 10000.0
EPS = 1e-6
N_MOD = 6

OFF_QA = 0
OFF_FA = 3 * FOX_WIDTH
OFF_QB = OFF_FA + FOX_HEADS
OFF_KB = OFF_QB + DSA_WIDTH
OFF_QI = OFF_KB + 2 * HEAD_DIM
OFF_KI = OFF_QI + IDX_WIDTH
OFF_WI = OFF_KI + IDX_DIM
D_IN = OFF_WI + IDX_HEADS

BIG_QB = 3 * FOX_WIDTH
BIG_QI = BIG_QB + DSA_WIDTH
BIG_N = BIG_QI + IDX_WIDTH
MISC_KI = 0
MISC_WI = IDX_DIM
MISC_FA = IDX_DIM + IDX_HEADS
SMALL_N = 3 * HEAD_DIM

LANES = 128
V7X_VMEM_BYTES = 64 * 1024 * 1024
VMEM_BUDGET = V7X_VMEM_BYTES - 8 * 1024 * 1024

DSA_IDX_TQ = 256
DSA_ATT_TQ = 128
DSA_TK = 256

NEG = -1e30
INT_MIN = -(2 ** 31)
LOG2E = 1.4426950408889634
LOGIT_SCALE = HEAD_DIM ** -0.5 * LOG2E


def _params(semantics, vmem_bytes):
    return pltpu.CompilerParams(dimension_semantics=semantics,
                                vmem_limit_bytes=int(min(vmem_bytes, VMEM_BUDGET)))


def _tile(n, pref):
    if n <= pref:
        return n
    t = (pref // LANES) * LANES
    while t > LANES and n % t:
        t -= LANES
    assert n % t == 0, (n, pref)
    return t


def _ada_kernel(c_ref, w_ref, b_ref, o_ref):
    c = c_ref[...]
    s = c * jax.nn.sigmoid(c)
    acc = jnp.dot(s.astype(BF16), w_ref[...].astype(BF16), preferred_element_type=F32)
    o_ref[...] = acc[None] + b_ref[...]


def _ada(c, w_ada, b_ada):
    B, D = c.shape
    L = b_ada.shape[0]
    N = w_ada.shape[1]
    rows = 8
    cp = jnp.zeros((rows, D), F32).at[:B].set(c)
    tn = _tile(N, 512)
    out = pl.pallas_call(
        _ada_kernel,
        grid=(N // tn,),
        in_specs=[pl.BlockSpec((rows, D), lambda j: (0, 0)),
                  pl.BlockSpec((D, tn), lambda j: (0, j)),
                  pl.BlockSpec((L, 1, tn), lambda j: (0, 0, j))],
        out_specs=pl.BlockSpec((L, rows, tn), lambda j: (0, 0, j)),
        out_shape=jax.ShapeDtypeStruct((L, rows, N), F32),
        compiler_params=_params(("parallel",), 4 * D * tn * 4 + (8 << 20)),
        name="ada",
    )(cp, w_ada, b_ada.reshape(L, 1, N))
    return out[:, :B].reshape(L, B, N_MOD, D)


def _rms(v, g):
    return v * lax.rsqrt(jnp.mean(v * v, axis=-1, keepdims=True) + EPS) * g


def _norm_kernel(*refs, resid, emit_h, gate_row, ga_row, scale_row, shift_row, gb_row):
    refs = list(refs)
    x_ref = refs.pop(0)
    y_ref = refs.pop(0) if resid else None
    moda_ref, modb_ref, ga_ref, gb_ref = refs[:4]
    outs = refs[4:]
    x = x_ref[0]
    if resid:
        gate = moda_ref[0, gate_row:gate_row + 1, :]
        x = x + gate * _rms(y_ref[0], ga_ref[ga_row:ga_row + 1, :])
        outs.pop(0)[0] = x
    if emit_h:
        scale = modb_ref[0, scale_row:scale_row + 1, :]
        shift = modb_ref[0, shift_row:shift_row + 1, :]
        h = _rms(x, gb_ref[gb_row:gb_row + 1, :]) * (1.0 + scale) + shift
        outs.pop(0)[0] = h.astype(BF16)


def _norm(x, y, moda, modb, ga, gb, *, gate_row=0, ga_row=0, scale_row=0, shift_row=0,
          gb_row=0, emit_h=True):
    B, S, D = x.shape
    resid = y is not None
    ts = _tile(S, 256)
    row = pl.BlockSpec((1, ts, D), lambda b, i: (b, i, 0))
    mod = pl.BlockSpec((1, N_MOD, D), lambda b, i: (b, 0, 0))
    gsp = pl.BlockSpec((4, D), lambda b, i: (0, 0))
    in_specs = [row] + ([row] if resid else []) + [mod, mod, gsp, gsp]
    args = [x] + ([y] if resid else []) + [moda, modb, ga, gb]
    out_shape, out_specs = [], []
    if resid:
        out_shape.append(jax.ShapeDtypeStruct((B, S, D), F32))
        out_specs.append(row)
    if emit_h:
        out_shape.append(jax.ShapeDtypeStruct((B, S, D), BF16))
        out_specs.append(row)
    n_rows_f32 = 1 + 2 * resid + 0.5 * emit_h
    outs = pl.pallas_call(
        functools.partial(_norm_kernel, resid=resid, emit_h=emit_h, gate_row=gate_row,
                          ga_row=ga_row, scale_row=scale_row, shift_row=shift_row,
                          gb_row=gb_row),
        grid=(B, S // ts),
        in_specs=in_specs, out_specs=out_specs, out_shape=out_shape,
        compiler_params=_params(("parallel", "parallel"),
                                2 * n_rows_f32 * ts * D * 4 + 6 * ts * D * 4 + (4 << 20)),
        name="norm",
    )(*args)
    outs = list(outs)
    x_new = outs.pop(0) if resid else None
    h = outs.pop(0) if emit_h else None
    return x_new, h


def _mm_kernel(*refs, n_a, pairs, n_extra, epilogue):
    n_w = len(pairs)
    a_refs = refs[:n_a]
    w_refs = refs[n_a:n_a + n_w]
    extra = refs[n_a + n_w:n_a + n_w + n_extra]
    outs = refs[n_a + n_w + n_extra:]
    accs = [jnp.dot(a_refs[ai][...], w_refs[wi][...], preferred_element_type=F32)
            for wi, ai in enumerate(pairs)]
    epilogue(accs, extra, outs)


def _matmul(a_list, w_list, extras, outs, epilogue, *, layer, tm, tn, n_tiles, name):
    M = a_list[0].shape[0]
    in_specs = [pl.BlockSpec((tm, a.shape[1]), lambda i, j: (i, 0)) for a in a_list]
    vmem = sum(2 * tm * a.shape[1] * a.dtype.itemsize for a in a_list)
    for _, w, off in w_list:
        in_specs.append(pl.BlockSpec((None, w.shape[1], tn),
                                     lambda i, j, off=off: (layer, 0, j + off)))
        vmem += 2 * w.shape[1] * tn * w.dtype.itemsize + 2 * tm * tn * 4
    in_specs += [sp for _, sp in extras]
    vmem += sum(2 * tm * LANES * 4 for _ in extras)
    vmem += sum(2 * tm * sp.block_shape[-1] * sd.dtype.itemsize for sd, sp in outs)
    return pl.pallas_call(
        functools.partial(_mm_kernel, n_a=len(a_list), pairs=tuple(ai for ai, _, _ in w_list),
                          n_extra=len(extras), epilogue=epilogue),
        grid=(M // tm, n_tiles),
        in_specs=in_specs,
        out_specs=[sp for _, sp in outs],
        out_shape=[sd for sd, _ in outs],
        compiler_params=_params(("parallel", "arbitrary"), vmem + (4 << 20)),
        name=name,
    )(*a_list, *[w for _, w, _ in w_list], *[e for e, _ in extras])


def _rope_slab(a, cosf, sinf, half):
    if 2 * half == LANES:
        partner = pltpu.roll(a, half, 1)
    else:
        lane = lax.broadcasted_iota(jnp.int32, a.shape, 1)
        partner = jnp.where(lane % (2 * half) < half,
                            pltpu.roll(a, LANES - half, 1), pltpu.roll(a, half, 1))
    return a * cosf + partner * sinf


def _epi_in_big(accs, extra, outs, *, tn):
    cos_h, sin_h, cos_i, sin_i = extra
    (o_ref,) = outs
    acc = accs[0]
    j = pl.program_id(1)
    t_ka, t_qb, t_qi = FOX_WIDTH // tn, BIG_QB // tn, BIG_QI // tn

    @pl.when(j < t_ka)
    def _():
        o_ref[...] = (acc * LOGIT_SCALE).astype(BF16)

    @pl.when((j >= t_ka) & (j < t_qb))
    def _():
        o_ref[...] = acc.astype(BF16)

    def roped(cos_ref, sin_ref, half):
        cosf, sinf = cos_ref[...], sin_ref[...]
        for s in range(tn // LANES):
            sl = slice(s * LANES, (s + 1) * LANES)
            o_ref[:, sl] = _rope_slab(acc[:, sl], cosf, sinf, half).astype(BF16)

    @pl.when((j >= t_qb) & (j < t_qi))
    def _():
        roped(cos_h, sin_h, HEAD_DIM // 2)

    @pl.when(j >= t_qi)
    def _():
        roped(cos_i, sin_i, IDX_DIM // 2)


def _epi_in_small(accs, extra, outs, *, idx_w_scale):
    cos_h, sin_h, cos_i, sin_i, fbias = extra
    kv_ref, misc_ref = outs
    acc = accs[0]
    kb = _rope_slab(acc[:, 0:LANES], cos_h[...], sin_h[...], HEAD_DIM // 2)
    kv_ref[:, 0:LANES] = kb.astype(BF16)
    kv_ref[:, LANES:2 * LANES] = acc[:, LANES:2 * LANES].astype(BF16)
    a = acc[:, 2 * LANES:3 * LANES]
    ki = _rope_slab(a, cos_i[...], sin_i[...], IDX_DIM // 2)
    f = a + fbias[...]
    log_f = jnp.minimum(f, 0.0) - jnp.log1p(jnp.exp(-jnp.abs(f)))
    lane = lax.broadcasted_iota(jnp.int32, a.shape, 1)
    misc_ref[...] = jnp.where(lane < MISC_WI, ki,
                              jnp.where(lane < MISC_FA, a * idx_w_scale, log_f))


def _epi_merge(accs, extra, outs):
    ga, gb, pa, pb = accs
    outs[0][...] = (jax.nn.sigmoid(ga) * pa + jax.nn.sigmoid(gb) * pb).astype(BF16)


def _epi_plain(accs, extra, outs):
    outs[0][...] = accs[0].astype(outs[0].dtype)


def _epi_swiglu(accs, extra, outs):
    g, u = accs
    outs[0][...] = (g * jax.nn.sigmoid(g) * u).astype(BF16)


def _cumsum_kernel(x_ref, o_ref, *, width):
    S = x_ref.shape[2]
    r = lax.broadcasted_iota(jnp.int32, (width, width), 0)
    c = lax.broadcasted_iota(jnp.int32, (width, width), 1)
    upper = jnp.where(r <= c, 1.0, 0.0).astype(BF16)
    carry = jnp.zeros((x_ref.shape[1], 1), F32)
    for i in range(S // width):
        x = x_ref[0, :, i * width:(i + 1) * width]
        hi = x.astype(BF16)
        r1 = x - hi.astype(F32)
        mid = r1.astype(BF16)
        lo = (r1 - mid.astype(F32)).astype(BF16)
        y = (jnp.dot(hi, upper, preferred_element_type=F32)
             + jnp.dot(mid, upper, preferred_element_type=F32)
             + jnp.dot(lo, upper, preferred_element_type=F32)) + carry
        o_ref[0, :, i * width:(i + 1) * width] = y
        carry = y[:, width - 1:width]


def _cumsum(x):
    B, H, S = x.shape
    width = _tile(S, 256)
    spec = pl.BlockSpec((1, H, S), lambda b: (b, 0, 0))
    return pl.pallas_call(
        functools.partial(_cumsum_kernel, width=width),
        grid=(B,), in_specs=[spec], out_specs=spec,
        out_shape=jax.ShapeDtypeStruct((B, H, S), F32),
        compiler_params=_params(("parallel",), 16 << 20),
        name="cumsum",
    )(x)


_NT = (((1,), (1,)), ((), ()))
_TN = (((0,), (0,)), ((), ()))


def _fox_kernel(q_ref, k_ref, v_ref, cum_ref, o_ref, ck_ref, m_ref, l_ref, acc_ref, s0_ref,
                *, tq, hg):
    g = pl.program_id(1)
    i = pl.program_id(2)
    S = k_ref.shape[0]
    D = HEAD_DIM

    @pl.when(i == 0)
    def _():
        lane = lax.broadcasted_iota(jnp.int32, (1, FOX_HEADS), 1)
        for c in range(S // tq):
            blk = cum_ref[0, c * tq:(c + 1) * tq, :]
            for j in range(hg):
                onehot = jnp.where(lane == g * hg + j, LOG2E, 0.0)
                col = jnp.sum(blk * onehot, axis=1, keepdims=True)
                ck_ref[j, c * tq:(c + 1) * tq, :] = jnp.broadcast_to(col, (tq, LANES))

    m_ref[...] = jnp.full(m_ref.shape, NEG, F32)
    l_ref[...] = jnp.zeros(l_ref.shape, F32)
    acc_ref[...] = jnp.zeros(acc_ref.shape, F32)

    def logits(c, j):
        start = pl.multiple_of(c * tq, tq)
        return lax.dot_general(k_ref[pl.ds(start, tq), j * D:(j + 1) * D],
                               q_ref[:, j * D:(j + 1) * D], _NT, preferred_element_type=F32)

    s0_ref[...] = logits(0, 0)

    def chunk(c, diagonal):
        start = pl.multiple_of(c * tq, tq)
        s_next = s0_ref[...]
        for j in range(hg):
            s = s_next
            if j + 1 < hg:
                s_next = logits(c, j + 1)
            elif not diagonal:
                s0_ref[...] = logits(c + 1, 0)
            v = v_ref[pl.ds(start, tq), j * D:(j + 1) * D]
            s = s - jnp.tile(ck_ref[j, pl.ds(start, tq), :], (1, tq // LANES))
            if diagonal:
                key = lax.broadcasted_iota(jnp.int32, s.shape, 0)
                qry = lax.broadcasted_iota(jnp.int32, s.shape, 1)
                s = jnp.where(key <= qry, s, NEG)
            m_prev = m_ref[j]
            m_new = jnp.maximum(m_prev, jnp.max(s, axis=0, keepdims=True))
            alpha = jnp.exp2(m_prev - m_new)
            p = jnp.exp2(s - m_new)
            l_ref[j] = alpha * l_ref[j] + jnp.sum(p, axis=0, keepdims=True)
            acc_ref[j] = alpha * acc_ref[j] + lax.dot_general(
                v, p.astype(BF16), _TN, preferred_element_type=F32)
            m_ref[j] = m_new

    def body(c, carry):
        chunk(c, False)
        return carry

    lax.fori_loop(0, i, body, 0)
    chunk(i, True)
    for j in range(hg):
        o_ref[:, j * D:(j + 1) * D] = (acc_ref[j] / l_ref[j]).T.astype(BF16)


def _fox(zb, cum, B, S):
    M = B * S
    tq = _tile(S, 512)
    nq = S // tq
    H = FOX_HEADS
    hg = 4
    G = H // hg
    W = hg * HEAD_DIM
    return pl.pallas_call(
        functools.partial(_fox_kernel, tq=tq, hg=hg),
        grid=(B, G, nq),
        in_specs=[pl.BlockSpec((tq, W), lambda b, g, i: (b * nq + i, g)),
                  pl.BlockSpec((S, W), lambda b, g, i: (b, G + g)),
                  pl.BlockSpec((S, W), lambda b, g, i: (b, 2 * G + g)),
                  pl.BlockSpec((1, S, H), lambda b, g, i: (b, 0, 0))],
        out_specs=pl.BlockSpec((tq, W), lambda b, g, i: (b * nq + i, g)),
        out_shape=jax.ShapeDtypeStruct((M, FOX_WIDTH), BF16),
        scratch_shapes=[pltpu.VMEM((hg, S, LANES), F32),
                        pltpu.VMEM((hg, 1, tq), F32), pltpu.VMEM((hg, 1, tq), F32),
                        pltpu.VMEM((hg, HEAD_DIM, tq), F32),
                        pltpu.VMEM((tq, tq), F32)],
        compiler_params=_params(("parallel", "parallel", "arbitrary"), 40 << 20),
        name="fox_attn",
    )(zb, zb, zb, cum)


def _dsa_index_kernel(qi_ref, ki_ref, wt_ref, bias_ref, qt_ref, key_ref, x_ref,
                      *, tq, tc, top_k, n_idx_bits):
    i = pl.program_id(1)
    S = ki_ref.shape[1]
    n_chunks = ((i + 1) * tq + tc - 1) // tc

    for p in range(IDX_WIDTH // LANES):
        blk = qi_ref[:, p * LANES:(p + 1) * LANES].astype(F32).T
        qt_ref[:, (2 * p) * tq:(2 * p + 1) * tq] = blk[:IDX_DIM].astype(BF16)
        qt_ref[:, (2 * p + 1) * tq:(2 * p + 2) * tq] = blk[IDX_DIM:].astype(BF16)
    bias_ref[...] = jnp.full(bias_ref.shape, NEG, BF16)

    t = i * tq + lax.broadcasted_iota(jnp.int32, (1, tq), 1)
    lim = (t // CHUNK + 1) * CHUNK

    def row_of(c):
        return c * tc + lax.broadcasted_iota(jnp.int32, (tc, tq), 0)

    def score_chunk(c, carry):
        start = pl.multiple_of(c * tc, tc)
        rel = jnp.dot(ki_ref[0, pl.ds(start, tc), :], qt_ref[...],
                      preferred_element_type=F32)
        acc = jnp.zeros((tc, tq), F32)
        for h in range(IDX_HEADS):
            acc = acc + wt_ref[0, h:h + 1, :] * jnp.maximum(rel[:, h * tq:(h + 1) * tq], 0.0)
        bits = pltpu.bitcast(acc, jnp.int32)
        key = bits ^ ((bits >> 31) & 0x7FFFFFFF)
        key = jnp.where(acc == 0.0, 0, key)
        key_ref[pl.ds(start, tc), :] = jnp.where(row_of(c) < lim, key, INT_MIN)
        return carry

    lax.fori_loop(0, n_chunks, score_chunk, 0)

    pair = 2 if (S // tc) % 2 == 0 else 1
    n_cnt = (n_chunks + pair - 1) // pair
    tcc = pair * tc
    if pair == 2:
        @pl.when(n_chunks % 2 == 1)
        def _():
            key_ref[pl.ds(pl.multiple_of(n_chunks * tc, tc), tc), :] = jnp.full(
                (tc, tq), INT_MIN, jnp.int32)

    def count(*preds):
        def body(c, cnts):
            kc = key_ref[pl.ds(pl.multiple_of(c * tcc, tcc), tcc), :]
            return tuple(cnt + jnp.sum(jnp.where(pred(kc, c), 1, 0).reshape(tcc // 32, 32, tq), axis=0)
                         for cnt, pred in zip(cnts, preds))
        cnts = lax.fori_loop(0, n_cnt, body,
                             tuple(jnp.zeros((32, tq), jnp.int32) for _ in preds))
        return [jnp.sum(cnt, axis=0, keepdims=True) for cnt in cnts]

    def cnt_row(c):
        return c * tcc + lax.broadcasted_iota(jnp.int32, (tcc, tq), 0)

    def bit_body(it, T):
        cand = T + lax.shift_left(jnp.int32(1), 31 - it)
        (n_ge,) = count(lambda kc, c: kc >= cand)
        return jnp.where(n_ge >= top_k, cand, T)

    T = lax.fori_loop(0, 32, bit_body, jnp.full((1, tq), INT_MIN, jnp.int32))
    n_ge, n_gt = count(lambda kc, c: kc >= T, lambda kc, c: kc > T)
    need = top_k - n_gt
    excess = jnp.where((n_ge > top_k) & (T != INT_MIN), 1, 0)
    x_ref[...] = jnp.full((1, tq), S, jnp.int32)

    @pl.when(jnp.max(excess) > 0)
    def _():
        def x_body(it, X):
            cand = X + lax.shift_left(jnp.int32(1), n_idx_bits - 1 - it)
            (n,) = count(lambda kc, c: (kc == T) & (cnt_row(c) < cand))
            return jnp.where(n <= need - 1, cand, X)
        x_ref[...] = lax.fori_loop(0, n_idx_bits, x_body, jnp.zeros((1, tq), jnp.int32))

    X = x_ref[...]

    def write_chunk(c, carry):
        start = pl.multiple_of(c * tc, tc)
        kc = key_ref[pl.ds(start, tc), :]
        row = row_of(c)
        tie = jnp.where(kc == T, jnp.where(row <= X, 0.0, NEG), NEG)
        sel = jnp.where(kc > T, 0.0, tie)
        bias_ref[0, 0, pl.ds(start, tc), :] = jnp.where(row < lim, sel, NEG).astype(BF16)
        return carry

    lax.fori_loop(0, n_chunks, write_chunk, 0)


def _dsa_index(zb, ki, wt, B, S, top_k):
    tq, tc = _tile(S, DSA_IDX_TQ), _tile(S, DSA_TK)
    nq = S // tq
    n_idx_bits = max(1, (S - 1).bit_length())
    return pl.pallas_call(
        functools.partial(_dsa_index_kernel, tq=tq, tc=tc, top_k=top_k, n_idx_bits=n_idx_bits),
        grid=(B, nq),
        in_specs=[pl.BlockSpec((tq, IDX_WIDTH), lambda b, i: (b * nq + i, BIG_QI // IDX_WIDTH)),
                  pl.BlockSpec((1, S, IDX_DIM), lambda b, i: (b, 0, 0)),
                  pl.BlockSpec((1, IDX_HEADS, tq), lambda b, i: (b, 0, i))],
        out_specs=pl.BlockSpec((1, 1, S, tq), lambda b, i: (b, i, 0, 0)),
        out_shape=jax.ShapeDtypeStruct((B, nq, S, tq), BF16),
        scratch_shapes=[pltpu.VMEM((IDX_DIM, IDX_HEADS * tq), BF16),
                        pltpu.VMEM((S, tq), jnp.int32),
                        pltpu.VMEM((1, tq), jnp.int32)],
        compiler_params=_params(("parallel", "arbitrary"), 40 << 20),
        name="dsa_index",
    )(zb, ki, wt)


def _dsa_attn_kernel(q_ref, k_ref, v_ref, bias_ref, o_ref, qt_ref, m_ref, l_ref, acc_ref, s_ref,
                     *, tq, tk, groups):
    i = pl.program_id(1)
    H = DSA_HEADS
    n_chunks = ((i + 1) * tq + tk - 1) // tk
    for h in range(H):
        qh = q_ref[:, h * HEAD_DIM:(h + 1) * HEAD_DIM].astype(F32)
        qt_ref[:, h * tq:(h + 1) * tq] = qh.T.astype(BF16)
    m_ref[...] = jnp.full(m_ref.shape, NEG, F32)
    l_ref[...] = jnp.zeros(l_ref.shape, F32)
    acc_ref[...] = jnp.zeros(acc_ref.shape, F32)

    gw = H * tq // groups

    n_total = k_ref.shape[0] // tk

    def stage(buf, c):
        start = pl.multiple_of(jnp.minimum(c, n_total - 1) * tk, tk)
        k = k_ref[pl.ds(start, tk), :]
        for g in range(groups):
            s_ref[buf, g] = jnp.dot(k, qt_ref[:, g * gw:(g + 1) * gw], preferred_element_type=F32)

    def consume(buf, c):
        start = pl.multiple_of(c * tk, tk)
        v = v_ref[pl.ds(start, tk), :]
        bias = bias_ref[0, 0, pl.ds(start, tk), :].astype(F32)
        bias = jnp.tile(bias, (1, gw // tq))
        for g in range(groups):
            sl = slice(g * gw, (g + 1) * gw)
            s = s_ref[buf, g] + bias
            m_prev = m_ref[:, sl]
            m_new = jnp.maximum(m_prev, jnp.max(s, axis=0, keepdims=True))
            alpha = jnp.exp2(m_prev - m_new)
            p = jnp.exp2(s - m_new)
            l_ref[:, sl] = alpha * l_ref[:, sl] + jnp.sum(p, axis=0, keepdims=True)
            acc_ref[:, sl] = alpha * acc_ref[:, sl] + lax.dot_general(
                v, p.astype(BF16), _TN, preferred_element_type=F32)
            m_ref[:, sl] = m_new

    if n_total % 2 == 0:
        stage(0, 0)

        def body(it, carry):
            stage(1, 2 * it + 1)
            consume(0, 2 * it)
            stage(0, 2 * it + 2)
            consume(1, 2 * it + 1)
            return carry

        lax.fori_loop(0, (n_chunks + 1) // 2, body, 0)
    else:
        def body(c, carry):
            stage(0, c)
            consume(0, c)
            return carry

        lax.fori_loop(0, n_chunks, body, 0)
    out = acc_ref[...] / l_ref[...]
    for h in range(H):
        o_ref[:, h * HEAD_DIM:(h + 1) * HEAD_DIM] = out[:, h * tq:(h + 1) * tq].T.astype(BF16)


def _dsa_attn(zb, kv, bias, B, S):
    M = B * S
    tq, tk = _tile(S, DSA_ATT_TQ), _tile(S, DSA_TK)
    nq = S // tq
    per_idx = bias.shape[3] // tq
    groups = 4
    return pl.pallas_call(
        functools.partial(_dsa_attn_kernel, tq=tq, tk=tk, groups=groups),
        grid=(B, nq),
        in_specs=[pl.BlockSpec((tq, DSA_WIDTH), lambda b, i: (b * nq + i, BIG_QB // DSA_WIDTH)),
                  pl.BlockSpec((S, HEAD_DIM), lambda b, i: (b, 0)),
                  pl.BlockSpec((S, HEAD_DIM), lambda b, i: (b, 1)),
                  pl.BlockSpec((1, 1, S, tq), lambda b, i: (b, i // per_idx, 0, i % per_idx))],
        out_specs=pl.BlockSpec((tq, DSA_WIDTH), lambda b, i: (b * nq + i, 0)),
        out_shape=jax.ShapeDtypeStruct((M, DSA_WIDTH), BF16),
        scratch_shapes=[pltpu.VMEM((HEAD_DIM, DSA_HEADS * tq), BF16),
                        pltpu.VMEM((1, DSA_HEADS * tq), F32),
                        pltpu.VMEM((1, DSA_HEADS * tq), F32),
                        pltpu.VMEM((HEAD_DIM, DSA_HEADS * tq), F32),
                        pltpu.VMEM((2, groups, tk, DSA_HEADS * tq // groups), F32)],
        compiler_params=_params(("parallel", "arbitrary"), 40 << 20),
        name="dsa_attn",
    )(zb, kv, kv, bias)


def _rope_tables(positions, dim):
    inv = ROPE_THETA ** (-jnp.arange(0, dim, 2, dtype=F32) / dim)
    ang = positions.astype(F32)[..., None] * inv
    cos, sin = jnp.cos(ang), jnp.sin(ang)
    reps = LANES // dim
    cosf = jnp.tile(jnp.concatenate([cos, cos], axis=-1), (1, 1, reps))
    sinf = jnp.tile(jnp.concatenate([-sin, sin], axis=-1), (1, 1, reps))
    return cosf.reshape(-1, LANES), sinf.reshape(-1, LANES)


def kernel(x, c, positions, w_ada, b_ada, norm_g, w_in, b_forget, w_proj_fox, w_proj_dsa,
           w_gate, w_out, w_ff_gate, w_ff_up, w_ff_down):
    B, S, D = x.shape
    M = B * S
    L = w_in.shape[0]
    F = w_ff_gate.shape[2]
    assert w_in.shape[2] == D_IN and S % CHUNK == 0
    top_k = min(TOPK_MAX, S // 4)
    idx_w_scale = IDX_WIDTH ** -0.5

    tm = _tile(M, 1024)
    tn_in = _tile(DSA_WIDTH, 1024)
    tn_merge = _tile(D, 256)
    tn_out = _tile(D, 1024)
    tn_ff = _tile(F, 512)
    tm_ff = _tile(M, 2048 if tn_ff <= 256 else 1024)
    tm_down, tn_down = _tile(M, 512), _tile(D, 512)

    wb16 = lambda w: w.astype(BF16)
    w_big = jnp.concatenate([wb16(w_in[:, :, OFF_QA:OFF_FA]), wb16(w_in[:, :, OFF_QB:OFF_KB]),
                             wb16(w_in[:, :, OFF_QI:OFF_KI])], axis=2)
    w_small = jnp.concatenate([wb16(w_in[:, :, OFF_KB:OFF_QI]), wb16(w_in[:, :, OFF_KI:D_IN]),
                               wb16(w_in[:, :, OFF_FA:OFF_QB]),
                               jnp.zeros((L, D, SMALL_N - 2 * HEAD_DIM - MISC_FA - FOX_HEADS), BF16)],
                              axis=2)
    w_gate16, w_pf16, w_pd16, w_out16 = wb16(w_gate), wb16(w_proj_fox), wb16(w_proj_dsa), wb16(w_out)
    w_fg16, w_fu16, w_fd16 = wb16(w_ff_gate), wb16(w_ff_up), wb16(w_ff_down)
    fbias = jnp.zeros((L, 1, LANES), F32).at[:, 0, MISC_FA:MISC_FA + FOX_HEADS].set(b_forget)

    cos_h, sin_h = _rope_tables(positions, HEAD_DIM)
    cos_i, sin_i = _rope_tables(positions, IDX_DIM)
    tab_spec = pl.BlockSpec((tm, LANES), lambda i, j: (i, 0))
    tables = [(t, tab_spec) for t in (cos_h, sin_h, cos_i, sin_i)]
    q_tables = [(t, tab_spec) for t in (cos_h * LOGIT_SCALE, sin_h * LOGIT_SCALE, cos_i, sin_i)]

    mods = _ada(c, w_ada, b_ada)

    _, h = _norm(x, None, mods[0], mods[0], norm_g[0], norm_g[0],
                 scale_row=1, shift_row=0, gb_row=0)
    for l in range(L):
        h2d = h.reshape(M, D)
        (zb,) = _matmul(
            [h2d], [(0, w_big, 0)], q_tables,
            [(jax.ShapeDtypeStruct((M, BIG_N), BF16), pl.BlockSpec((tm, tn_in), lambda i, j: (i, j)))],
            functools.partial(_epi_in_big, tn=tn_in), layer=l, tm=tm, tn=tn_in, n_tiles=BIG_N // tn_in,
            name="in_proj")
        kv, misc = _matmul(
            [h2d], [(0, w_small, 0)],
            tables + [(fbias[l], pl.BlockSpec((1, LANES), lambda i, j: (0, 0)))],
            [(jax.ShapeDtypeStruct((M, 2 * HEAD_DIM), BF16),
              pl.BlockSpec((tm, 2 * HEAD_DIM), lambda i, j: (i, 0))),
             (jax.ShapeDtypeStruct((M, LANES), F32), pl.BlockSpec((tm, LANES), lambda i, j: (i, 0)))],
            functools.partial(_epi_in_small, idx_w_scale=idx_w_scale), layer=l, tm=tm, tn=SMALL_N, n_tiles=1,
            name="in_proj_small")
        misc3 = misc.reshape(B, S, LANES)
        ki = misc3[:, :, MISC_KI:MISC_KI + IDX_DIM].astype(BF16)
        w_it = misc3[:, :, MISC_WI:MISC_WI + IDX_HEADS].transpose(0, 2, 1)
        log_f = misc3[:, :, MISC_FA:MISC_FA + FOX_HEADS].transpose(0, 2, 1)
        cum = _cumsum(log_f).transpose(0, 2, 1)

        y_a = _fox(zb, cum, B, S)
        bias = _dsa_index(zb, ki, w_it, B, S, top_k)
        y_b = _dsa_attn(zb, kv, bias, B, S)

        n_gate_tiles = D // tn_merge
        (mixed,) = _matmul(
            [h2d, y_a, y_b],
            [(0, w_gate16, 0), (0, w_gate16, n_gate_tiles), (1, w_pf16, 0), (2, w_pd16, 0)],
            [], [(jax.ShapeDtypeStruct((M, D), BF16), pl.BlockSpec((tm, tn_merge), lambda i, j: (i, j)))],
            _epi_merge, layer=l, tm=tm, tn=tn_merge, n_tiles=n_gate_tiles, name="gate_merge")
        (yo,) = _matmul(
            [mixed], [(0, w_out16, 0)], [],
            [(jax.ShapeDtypeStruct((M, D), F32), pl.BlockSpec((tm, tn_out), lambda i, j: (i, j)))],
            _epi_plain, layer=l, tm=tm, tn=tn_out, n_tiles=D // tn_out, name="out_proj")
        x, h = _norm(x, yo.reshape(B, S, D), mods[l], mods[l], norm_g[l], norm_g[l],
                     gate_row=2, ga_row=1, scale_row=4, shift_row=3, gb_row=2)

        (t,) = _matmul(
            [h.reshape(M, D)], [(0, w_fg16, 0), (0, w_fu16, 0)], [],
            [(jax.ShapeDtypeStruct((M, F), BF16), pl.BlockSpec((tm_ff, tn_ff), lambda i, j: (i, j)))],
            _epi_swiglu, layer=l, tm=tm_ff, tn=tn_ff, n_tiles=F // tn_ff, name="ffn_gate_up")
        (ff,) = _matmul(
            [t], [(0, w_fd16, 0)], [],
            [(jax.ShapeDtypeStruct((M, D), F32), pl.BlockSpec((tm_down, tn_down), lambda i, j: (i, j)))],
            _epi_plain, layer=l, tm=tm_down, tn=tn_down, n_tiles=D // tn_down, name="ffn_down")
        last = l == L - 1
        nl = l if last else l + 1
        x, h = _norm(x, ff.reshape(B, S, D), mods[l], mods[nl], norm_g[l], norm_g[nl],
                     gate_row=5, ga_row=3, scale_row=1, shift_row=0, gb_row=0, emit_h=not last)
    return x
```

```python
import functools

import jax
import jax.numpy as jnp
from jax import lax
from jax.experimental import pallas as pl
from jax.experimental.pallas import tpu as pltpu

F32 = jnp.float32
BF16 = jnp.bfloat16

CHUNK = 64
HEAD_DIM = 128
FOX_HEADS = 16
DSA_HEADS = 16
FOX_WIDTH = FOX_HEADS * HEAD_DIM
DSA_WIDTH = DSA_HEADS * HEAD_DIM
IDX_HEADS = 32
IDX_DIM = 64
IDX_WIDTH = IDX_HEADS * IDX_DIM
TOPK_MAX = 256
ROPE_THETA = 10000.0
EPS = 1e-6
N_MOD = 6

OFF_QA = 0
OFF_FA = 3 * FOX_WIDTH
OFF_QB = OFF_FA + FOX_HEADS
OFF_KB = OFF_QB + DSA_WIDTH
OFF_QI = OFF_KB + 2 * HEAD_DIM
OFF_KI = OFF_QI + IDX_WIDTH
OFF_WI = OFF_KI + IDX_DIM
D_IN = OFF_WI + IDX_HEADS

MISC_KI = 0
MISC_WI = IDX_DIM
MISC_FA = IDX_DIM + IDX_HEADS
SMALL_N = 3 * HEAD_DIM

LANES = 128
V7X_VMEM_BYTES = 64 * 1024 * 1024
VMEM_BUDGET = V7X_VMEM_BYTES - 8 * 1024 * 1024

DSA_IDX_TQ = 256
DSA_ATT_TQ = 128
DSA_TK = 256

NEG = -1e30
INT_MIN = -(2 ** 31)
LOG2E = 1.4426950408889634
LOGIT_SCALE = HEAD_DIM ** -0.5 * LOG2E


def _params(semantics, vmem_bytes):
    return pltpu.CompilerParams(dimension_semantics=semantics,
                                vmem_limit_bytes=int(min(vmem_bytes, VMEM_BUDGET)))


def _tile(n, pref):
    if n <= pref:
        return n
    t = (pref // LANES) * LANES
    while t > LANES and n % t:
        t -= LANES
    assert n % t == 0, (n, pref)
    return t


def _ada_kernel(c_ref, w_ref, b_ref, o_ref):
    c = c_ref[...]
    s = c * jax.nn.sigmoid(c)
    acc = jnp.dot(s.astype(BF16), w_ref[...].astype(BF16), preferred_element_type=F32)
    o_ref[...] = acc[None] + b_ref[...]


def _ada(c, w_ada, b_ada):
    B, D = c.shape
    L = b_ada.shape[0]
    N = w_ada.shape[1]
    rows = 8
    cp = jnp.zeros((rows, D), F32).at[:B].set(c)
    tn = _tile(N, 512)
    out = pl.pallas_call(
        _ada_kernel,
        grid=(N // tn,),
        in_specs=[pl.BlockSpec((rows, D), lambda j: (0, 0)),
                  pl.BlockSpec((D, tn), lambda j: (0, j)),
                  pl.BlockSpec((L, 1, tn), lambda j: (0, 0, j))],
        out_specs=pl.BlockSpec((L, rows, tn), lambda j: (0, 0, j)),
        out_shape=jax.ShapeDtypeStruct((L, rows, N), F32),
        compiler_params=_params(("parallel",), 4 * D * tn * 4 + (8 << 20)),
        name="ada",
    )(cp, w_ada, b_ada.reshape(L, 1, N))
    return out[:, :B].reshape(L, B, N_MOD, D)


def _rms(v, g):
    return v * lax.rsqrt(jnp.mean(v * v, axis=-1, keepdims=True) + EPS) * g


def _norm_kernel(*refs, resid, emit_h, gate_row, ga_row, scale_row, shift_row, gb_row):
    refs = list(refs)
    x_ref = refs.pop(0)
    y_ref = refs.pop(0) if resid else None
    moda_ref, modb_ref, ga_ref, gb_ref = refs[:4]
    outs = refs[4:]
    x = x_ref[0]
    if resid:
        gate = moda_ref[0, gate_row:gate_row + 1, :]
        x = x + gate * _rms(y_ref[0], ga_ref[ga_row:ga_row + 1, :])
        outs.pop(0)[0] = x
    if emit_h:
        scale = modb_ref[0, scale_row:scale_row + 1, :]
        shift = modb_ref[0, shift_row:shift_row + 1, :]
        h = _rms(x, gb_ref[gb_row:gb_row + 1, :]) * (1.0 + scale) + shift
        outs.pop(0)[0] = h.astype(BF16)


def _norm(x, y, moda, modb, ga, gb, *, gate_row=0, ga_row=0, scale_row=0, shift_row=0,
          gb_row=0, emit_h=True):
    B, S, D = x.shape
    resid = y is not None
    ts = _tile(S, 256)
    row = pl.BlockSpec((1, ts, D), lambda b, i: (b, i, 0))
    mod = pl.BlockSpec((1, N_MOD, D), lambda b, i: (b, 0, 0))
    gsp = pl.BlockSpec((4, D), lambda b, i: (0, 0))
    in_specs = [row] + ([row] if resid else []) + [mod, mod, gsp, gsp]
    args = [x] + ([y] if resid else []) + [moda, modb, ga, gb]
    out_shape, out_specs = [], []
    if resid:
        out_shape.append(jax.ShapeDtypeStruct((B, S, D), F32))
        out_specs.append(row)
    if emit_h:
        out_shape.append(jax.ShapeDtypeStruct((B, S, D), BF16))
        out_specs.append(row)
    n_rows_f32 = 1 + 2 * resid + 0.5 * emit_h
    outs = pl.pallas_call(
        functools.partial(_norm_kernel, resid=resid, emit_h=emit_h, gate_row=gate_row,
                          ga_row=ga_row, scale_row=scale_row, shift_row=shift_row,
                          gb_row=gb_row),
        grid=(B, S // ts),
        in_specs=in_specs, out_specs=out_specs, out_shape=out_shape,
        compiler_params=_params(("parallel", "parallel"),
                                2 * n_rows_f32 * ts * D * 4 + 6 * ts * D * 4 + (4 << 20)),
        name="norm",
    )(*args)
    outs = list(outs)
    x_new = outs.pop(0) if resid else None
    h = outs.pop(0) if emit_h else None
    return x_new, h


def _mm_kernel(*refs, n_a, pairs, n_extra, epilogue):
    n_w = len(pairs)
    a_refs = refs[:n_a]
    w_refs = refs[n_a:n_a + n_w]
    extra = refs[n_a + n_w:n_a + n_w + n_extra]
    outs = refs[n_a + n_w + n_extra:]
    accs = [jnp.dot(a_refs[ai][...], w_refs[wi][...], preferred_element_type=F32)
            for wi, ai in enumerate(pairs)]
    epilogue(accs, extra, outs)


def _matmul(a_list, w_list, extras, outs, epilogue, *, layer, tm, tn, n_tiles, name):
    M = a_list[0].shape[0]
    in_specs = [pl.BlockSpec((tm, a.shape[1]), lambda i, j: (i, 0)) for a in a_list]
    vmem = sum(2 * tm * a.shape[1] * a.dtype.itemsize for a in a_list)
    for _, w, off in w_list:
        in_specs.append(pl.BlockSpec((None, w.shape[1], tn),
                                     lambda i, j, off=off: (layer, 0, j + off)))
        vmem += 2 * w.shape[1] * tn * w.dtype.itemsize + 2 * tm * tn * 4
    in_specs += [sp for _, sp in extras]
    vmem += sum(2 * tm * LANES * 4 for _ in extras)
    vmem += sum(2 * tm * sp.block_shape[-1] * sd.dtype.itemsize for sd, sp in outs)
    return pl.pallas_call(
        functools.partial(_mm_kernel, n_a=len(a_list), pairs=tuple(ai for ai, _, _ in w_list),
                          n_extra=len(extras), epilogue=epilogue),
        grid=(M // tm, n_tiles),
        in_specs=in_specs,
        out_specs=[sp for _, sp in outs],
        out_shape=[sd for sd, _ in outs],
        compiler_params=_params(("parallel", "arbitrary"), vmem + (4 << 20)),
        name=name,
    )(*a_list, *[w for _, w, _ in w_list], *[e for e, _ in extras])


def _rope_slab(a, cosf, sinf, half):
    if 2 * half == LANES:
        partner = pltpu.roll(a, half, 1)
    else:
        lane = lax.broadcasted_iota(jnp.int32, a.shape, 1)
        partner = jnp.where(lane % (2 * half) < half,
                            pltpu.roll(a, LANES - half, 1), pltpu.roll(a, half, 1))
    return a * cosf + partner * sinf


def _mm_ws_kernel(a_ref, w_ref, *rest, n_extra, epilogue):
    extra, o_ref, wb_ref = rest[:n_extra], rest[n_extra], rest[n_extra + 1]

    @pl.when(pl.program_id(1) == 0)
    def _():
        wb_ref[...] = w_ref[...].astype(BF16)

    acc = jnp.dot(a_ref[...], wb_ref[...], preferred_element_type=F32)
    epilogue(acc, extra, o_ref, pl.program_id(0))


def _matmul_ws(a, w, extras, n_out, epilogue, *, layer, col0, tm, tn, name):
    M, K = a.shape
    row = lambda j, i: (i, 0)
    vmem = (2 * tm * K * 2 + K * tn * 4 + K * tn * 2 + 2 * tm * tn * 2 + tm * tn * 4
            + len(extras) * 2 * tm * LANES * 4)
    return pl.pallas_call(
        functools.partial(_mm_ws_kernel, n_extra=len(extras), epilogue=epilogue),
        grid=(n_out // tn, M // tm),
        in_specs=[pl.BlockSpec((tm, K), row),
                  pl.BlockSpec((None, K, tn), lambda j, i: (layer, 0, j + col0),
                               pipeline_mode=pl.Buffered(1))]
                 + [pl.BlockSpec((tm, LANES), row) for _ in extras],
        out_specs=pl.BlockSpec((tm, tn), lambda j, i: (i, j)),
        out_shape=jax.ShapeDtypeStruct((M, n_out), BF16),
        scratch_shapes=[pltpu.VMEM((K, tn), BF16)],
        compiler_params=_params(("arbitrary", "arbitrary"), vmem + (8 << 20)),
        name=name,
    )(a, w, *extras)


def _epi_fox_qkv(acc, extra, o_ref, j, *, tn):
    @pl.when(j < FOX_WIDTH // tn)
    def _():
        o_ref[...] = (acc * LOGIT_SCALE).astype(BF16)

    @pl.when(j >= FOX_WIDTH // tn)
    def _():
        o_ref[...] = acc.astype(BF16)


def _epi_rope(acc, extra, o_ref, j, *, half):
    cos_ref, sin_ref = extra
    cosf, sinf = cos_ref[...], sin_ref[...]
    for s in range(acc.shape[1] // LANES):
        sl = slice(s * LANES, (s + 1) * LANES)
        o_ref[:, sl] = _rope_slab(acc[:, sl], cosf, sinf, half).astype(BF16)


def _epi_in_small(accs, extra, outs, *, idx_w_scale):
    cos_h, sin_h, cos_i, sin_i, fbias = extra
    kv_ref, misc_ref = outs
    acc = accs[0]
    kb = _rope_slab(acc[:, 0:LANES], cos_h[...], sin_h[...], HEAD_DIM // 2)
    kv_ref[:, 0:LANES] = kb.astype(BF16)
    kv_ref[:, LANES:2 * LANES] = acc[:, LANES:2 * LANES].astype(BF16)
    a = acc[:, 2 * LANES:3 * LANES]
    ki = _rope_slab(a, cos_i[...], sin_i[...], IDX_DIM // 2)
    f = a + fbias[...]
    log_f = jnp.minimum(f, 0.0) - jnp.log1p(jnp.exp(-jnp.abs(f)))
    lane = lax.broadcasted_iota(jnp.int32, a.shape, 1)
    misc_ref[...] = jnp.where(lane < MISC_WI, ki,
                              jnp.where(lane < MISC_FA, a * idx_w_scale, log_f))


def _epi_merge(accs, extra, outs):
    ga, gb, pa, pb = accs
    outs[0][...] = (jax.nn.sigmoid(ga) * pa + jax.nn.sigmoid(gb) * pb).astype(BF16)


def _epi_plain(accs, extra, outs):
    outs[0][...] = accs[0].astype(outs[0].dtype)


def _epi_swiglu(accs, extra, outs):
    g, u = accs
    outs[0][...] = (g * jax.nn.sigmoid(g) * u).astype(BF16)


def _cumsum_kernel(x_ref, o_ref, *, width):
    S = x_ref.shape[2]
    r = lax.broadcasted_iota(jnp.int32, (width, width), 0)
    c = lax.broadcasted_iota(jnp.int32, (width, width), 1)
    upper = jnp.where(r <= c, 1.0, 0.0).astype(BF16)
    carry = jnp.zeros((x_ref.shape[1], 1), F32)
    for i in range(S // width):
        x = x_ref[0, :, i * width:(i + 1) * width]
        hi = x.astype(BF16)
        r1 = x - hi.astype(F32)
        mid = r1.astype(BF16)
        lo = (r1 - mid.astype(F32)).astype(BF16)
        y = (jnp.dot(hi, upper, preferred_element_type=F32)
             + jnp.dot(mid, upper, preferred_element_type=F32)
             + jnp.dot(lo, upper, preferred_element_type=F32)) + carry
        o_ref[0, :, i * width:(i + 1) * width] = y
        carry = y[:, width - 1:width]


def _cumsum(x):
    B, H, S = x.shape
    width = _tile(S, 256)
    spec = pl.BlockSpec((1, H, S), lambda b: (b, 0, 0))
    return pl.pallas_call(
        functools.partial(_cumsum_kernel, width=width),
        grid=(B,), in_specs=[spec], out_specs=spec,
        out_shape=jax.ShapeDtypeStruct((B, H, S), F32),
        compiler_params=_params(("parallel",), 16 << 20),
        name="cumsum",
    )(x)


_NT = (((1,), (1,)), ((), ()))
_TN = (((0,), (0,)), ((), ()))


def _fox_kernel(q_ref, k_ref, v_ref, cum_ref, o_ref, ck_ref, m_ref, l_ref, acc_ref, s0_ref,
                *, tq, hg):
    g = pl.program_id(1)
    i = pl.program_id(2)
    S = k_ref.shape[0]
    D = HEAD_DIM

    @pl.when(i == 0)
    def _():
        lane = lax.broadcasted_iota(jnp.int32, (1, FOX_HEADS), 1)
        for c in range(S // tq):
            blk = cum_ref[0, c * tq:(c + 1) * tq, :]
            for j in range(hg):
                onehot = jnp.where(lane == g * hg + j, LOG2E, 0.0)
                col = jnp.sum(blk * onehot, axis=1, keepdims=True)
                ck_ref[j, c * tq:(c + 1) * tq, :] = jnp.broadcast_to(col, (tq, LANES))

    m_ref[...] = jnp.full(m_ref.shape, NEG, F32)
    l_ref[...] = jnp.zeros(l_ref.shape, F32)
    acc_ref[...] = jnp.zeros(acc_ref.shape, F32)

    def logits(c, j):
        start = pl.multiple_of(c * tq, tq)
        return lax.dot_general(k_ref[pl.ds(start, tq), j * D:(j + 1) * D],
                               q_ref[:, j * D:(j + 1) * D], _NT, preferred_element_type=F32)

    s0_ref[...] = logits(0, 0)

    def chunk(c, diagonal):
        start = pl.multiple_of(c * tq, tq)
        s_next = s0_ref[...]
        for j in range(hg):
            s = s_next
            if j + 1 < hg:
                s_next = logits(c, j + 1)
            elif not diagonal:
                s0_ref[...] = logits(c + 1, 0)
            v = v_ref[pl.ds(start, tq), j * D:(j + 1) * D]
            s = s - jnp.tile(ck_ref[j, pl.ds(start, tq), :], (1, tq // LANES))
            if diagonal:
                key = lax.broadcasted_iota(jnp.int32, s.shape, 0)
                qry = lax.broadcasted_iota(jnp.int32, s.shape, 1)
                s = jnp.where(key <= qry, s, NEG)
            m_prev = m_ref[j]
            m_new = jnp.maximum(m_prev, jnp.max(s, axis=0, keepdims=True))
            alpha = jnp.exp2(m_prev - m_new)
            p = jnp.exp2(s - m_new)
            l_ref[j] = alpha * l_ref[j] + jnp.sum(p, axis=0, keepdims=True)
            acc_ref[j] = alpha * acc_ref[j] + lax.dot_general(
                v, p.astype(BF16), _TN, preferred_element_type=F32)
            m_ref[j] = m_new

    def body(c, carry):
        chunk(c, False)
        return carry

    lax.fori_loop(0, i, body, 0)
    chunk(i, True)
    for j in range(hg):
        o_ref[:, j * D:(j + 1) * D] = (acc_ref[j] / l_ref[j]).T.astype(BF16)


def _fox(zb, cum, B, S):
    M = B * S
    tq = _tile(S, 512)
    nq = S // tq
    H = FOX_HEADS
    hg = 4
    G = H // hg
    W = hg * HEAD_DIM
    return pl.pallas_call(
        functools.partial(_fox_kernel, tq=tq, hg=hg),
        grid=(B, G, nq),
        in_specs=[pl.BlockSpec((tq, W), lambda b, g, i: (b * nq + i, g)),
                  pl.BlockSpec((S, W), lambda b, g, i: (b, G + g)),
                  pl.BlockSpec((S, W), lambda b, g, i: (b, 2 * G + g)),
                  pl.BlockSpec((1, S, H), lambda b, g, i: (b, 0, 0))],
        out_specs=pl.BlockSpec((tq, W), lambda b, g, i: (b * nq + i, g)),
        out_shape=jax.ShapeDtypeStruct((M, FOX_WIDTH), BF16),
        scratch_shapes=[pltpu.VMEM((hg, S, LANES), F32),
                        pltpu.VMEM((hg, 1, tq), F32), pltpu.VMEM((hg, 1, tq), F32),
                        pltpu.VMEM((hg, HEAD_DIM, tq), F32),
                        pltpu.VMEM((tq, tq), F32)],
        compiler_params=_params(("parallel", "parallel", "arbitrary"), 40 << 20),
        name="fox_attn",
    )(zb, zb, zb, cum)


def _dsa_index_kernel(qi_ref, ki_ref, wt_ref, bias_ref, qt_ref, key_ref, x_ref,
                      *, tq, tc, top_k, n_idx_bits):
    i = pl.program_id(1)
    S = ki_ref.shape[1]
    n_chunks = ((i + 1) * tq + tc - 1) // tc

    for p in range(IDX_WIDTH // LANES):
        blk = qi_ref[:, p * LANES:(p + 1) * LANES].astype(F32).T
        qt_ref[:, (2 * p) * tq:(2 * p + 1) * tq] = blk[:IDX_DIM].astype(BF16)
        qt_ref[:, (2 * p + 1) * tq:(2 * p + 2) * tq] = blk[IDX_DIM:].astype(BF16)
    bias_ref[...] = jnp.full(bias_ref.shape, NEG, BF16)

    t = i * tq + lax.broadcasted_iota(jnp.int32, (1, tq), 1)
    lim = (t // CHUNK + 1) * CHUNK

    def row_of(c):
        return c * tc + lax.broadcasted_iota(jnp.int32, (tc, tq), 0)

    def score_chunk(c, carry):
        start = pl.multiple_of(c * tc, tc)
        rel = jnp.dot(ki_ref[0, pl.ds(start, tc), :], qt_ref[...],
                      preferred_element_type=F32)
        acc = jnp.zeros((tc, tq), F32)
        for h in range(IDX_HEADS):
            acc = acc + wt_ref[0, h:h + 1, :] * jnp.maximum(rel[:, h * tq:(h + 1) * tq], 0.0)
        bits = pltpu.bitcast(acc, jnp.int32)
        key = bits ^ ((bits >> 31) & 0x7FFFFFFF)
        key = jnp.where(acc == 0.0, 0, key)
        key_ref[pl.ds(start, tc), :] = jnp.where(row_of(c) < lim, key, INT_MIN)
        return carry

    lax.fori_loop(0, n_chunks, score_chunk, 0)

    pair = 2 if (S // tc) % 2 == 0 else 1
    n_cnt = (n_chunks + pair - 1) // pair
    tcc = pair * tc
    if pair == 2:
        @pl.when(n_chunks % 2 == 1)
        def _():
            key_ref[pl.ds(pl.multiple_of(n_chunks * tc, tc), tc), :] = jnp.full(
                (tc, tq), INT_MIN, jnp.int32)

    def count(*preds):
        def body(c, cnts):
            kc = key_ref[pl.ds(pl.multiple_of(c * tcc, tcc), tcc), :]
            return tuple(cnt + jnp.sum(jnp.where(pred(kc, c), 1, 0).reshape(tcc // 32, 32, tq), axis=0)
                         for cnt, pred in zip(cnts, preds))
        cnts = lax.fori_loop(0, n_cnt, body,
                             tuple(jnp.zeros((32, tq), jnp.int32) for _ in preds))
        return [jnp.sum(cnt, axis=0, keepdims=True) for cnt in cnts]

    def cnt_row(c):
        return c * tcc + lax.broadcasted_iota(jnp.int32, (tcc, tq), 0)

    def bit_body(it, T):
        cand = T + lax.shift_left(jnp.int32(1), 31 - it)
        (n_ge,) = count(lambda kc, c: kc >= cand)
        return jnp.where(n_ge >= top_k, cand, T)

    T = lax.fori_loop(0, 32, bit_body, jnp.full((1, tq), INT_MIN, jnp.int32))
    n_ge, n_gt = count(lambda kc, c: kc >= T, lambda kc, c: kc > T)
    need = top_k - n_gt
    excess = jnp.where((n_ge > top_k) & (T != INT_MIN), 1, 0)
    x_ref[...] = jnp.full((1, tq), S, jnp.int32)

    @pl.when(jnp.max(excess) > 0)
    def _():
        def x_body(it, X):
            cand = X + lax.shift_left(jnp.int32(1), n_idx_bits - 1 - it)
            (n,) = count(lambda kc, c: (kc == T) & (cnt_row(c) < cand))
            return jnp.where(n <= need - 1, cand, X)
        x_ref[...] = lax.fori_loop(0, n_idx_bits, x_body, jnp.zeros((1, tq), jnp.int32))

    X = x_ref[...]

    def write_chunk(c, carry):
        start = pl.multiple_of(c * tc, tc)
        kc = key_ref[pl.ds(start, tc), :]
        row = row_of(c)
        tie = jnp.where(kc == T, jnp.where(row <= X, 0.0, NEG), NEG)
        sel = jnp.where(kc > T, 0.0, tie)
        bias_ref[0, 0, pl.ds(start, tc), :] = jnp.where(row < lim, sel, NEG).astype(BF16)
        return carry

    lax.fori_loop(0, n_chunks, write_chunk, 0)


def _dsa_index(zb, ki, wt, B, S, top_k):
    tq, tc = _tile(S, DSA_IDX_TQ), _tile(S, DSA_TK)
    nq = S // tq
    n_idx_bits = max(1, (S - 1).bit_length())
    return pl.pallas_call(
        functools.partial(_dsa_index_kernel, tq=tq, tc=tc, top_k=top_k, n_idx_bits=n_idx_bits),
        grid=(B, nq),
        in_specs=[pl.BlockSpec((tq, IDX_WIDTH), lambda b, i: (b * nq + i, 0)),
                  pl.BlockSpec((1, S, IDX_DIM), lambda b, i: (b, 0, 0)),
                  pl.BlockSpec((1, IDX_HEADS, tq), lambda b, i: (b, 0, i))],
        out_specs=pl.BlockSpec((1, 1, S, tq), lambda b, i: (b, i, 0, 0)),
        out_shape=jax.ShapeDtypeStruct((B, nq, S, tq), BF16),
        scratch_shapes=[pltpu.VMEM((IDX_DIM, IDX_HEADS * tq), BF16),
                        pltpu.VMEM((S, tq), jnp.int32),
                        pltpu.VMEM((1, tq), jnp.int32)],
        compiler_params=_params(("parallel", "arbitrary"), 40 << 20),
        name="dsa_index",
    )(zb, ki, wt)


def _dsa_attn_kernel(q_ref, k_ref, v_ref, bias_ref, o_ref, qt_ref, m_ref, l_ref, acc_ref, s_ref,
                     *, tq, tk, groups):
    i = pl.program_id(1)
    H = DSA_HEADS
    n_chunks = ((i + 1) * tq + tk - 1) // tk
    for h in range(H):
        qh = q_ref[:, h * HEAD_DIM:(h + 1) * HEAD_DIM].astype(F32)
        qt_ref[:, h * tq:(h + 1) * tq] = qh.T.astype(BF16)
    m_ref[...] = jnp.full(m_ref.shape, NEG, F32)
    l_ref[...] = jnp.zeros(l_ref.shape, F32)
    acc_ref[...] = jnp.zeros(acc_ref.shape, F32)

    gw = H * tq // groups

    n_total = k_ref.shape[0] // tk

    def stage(buf, c):
        start = pl.multiple_of(jnp.minimum(c, n_total - 1) * tk, tk)
        k = k_ref[pl.ds(start, tk), :]
        for g in range(groups):
            s_ref[buf, g] = jnp.dot(k, qt_ref[:, g * gw:(g + 1) * gw], preferred_element_type=F32)

    def consume(buf, c):
        start = pl.multiple_of(c * tk, tk)
        v = v_ref[pl.ds(start, tk), :]
        bias = bias_ref[0, 0, pl.ds(start, tk), :].astype(F32)
        bias = jnp.tile(bias, (1, gw // tq))
        for g in range(groups):
            sl = slice(g * gw, (g + 1) * gw)
            s = s_ref[buf, g] + bias
            m_prev = m_ref[:, sl]
            m_new = jnp.maximum(m_prev, jnp.max(s, axis=0, keepdims=True))
            alpha = jnp.exp2(m_prev - m_new)
            p = jnp.exp2(s - m_new)
            l_ref[:, sl] = alpha * l_ref[:, sl] + jnp.sum(p, axis=0, keepdims=True)
            acc_ref[:, sl] = alpha * acc_ref[:, sl] + lax.dot_general(
                v, p.astype(BF16), _TN, preferred_element_type=F32)
            m_ref[:, sl] = m_new

    if n_total % 2 == 0:
        stage(0, 0)

        def body(it, carry):
            stage(1, 2 * it + 1)
            consume(0, 2 * it)
            stage(0, 2 * it + 2)
            consume(1, 2 * it + 1)
            return carry

        lax.fori_loop(0, (n_chunks + 1) // 2, body, 0)
    else:
        def body(c, carry):
            stage(0, c)
            consume(0, c)
            return carry

        lax.fori_loop(0, n_chunks, body, 0)
    out = acc_ref[...] / l_ref[...]
    for h in range(H):
        o_ref[:, h * HEAD_DIM:(h + 1) * HEAD_DIM] = out[:, h * tq:(h + 1) * tq].T.astype(BF16)


def _dsa_attn(zb, kv, bias, B, S):
    M = B * S
    tq, tk = _tile(S, DSA_ATT_TQ), _tile(S, DSA_TK)
    nq = S // tq
    per_idx = bias.shape[3] // tq
    groups = 4
    return pl.pallas_call(
        functools.partial(_dsa_attn_kernel, tq=tq, tk=tk, groups=groups),
        grid=(B, nq),
        in_specs=[pl.BlockSpec((tq, DSA_WIDTH), lambda b, i: (b * nq + i, 0)),
                  pl.BlockSpec((S, HEAD_DIM), lambda b, i: (b, 0)),
                  pl.BlockSpec((S, HEAD_DIM), lambda b, i: (b, 1)),
                  pl.BlockSpec((1, 1, S, tq), lambda b, i: (b, i // per_idx, 0, i % per_idx))],
        out_specs=pl.BlockSpec((tq, DSA_WIDTH), lambda b, i: (b * nq + i, 0)),
        out_shape=jax.ShapeDtypeStruct((M, DSA_WIDTH), BF16),
        scratch_shapes=[pltpu.VMEM((HEAD_DIM, DSA_HEADS * tq), BF16),
                        pltpu.VMEM((1, DSA_HEADS * tq), F32),
                        pltpu.VMEM((1, DSA_HEADS * tq), F32),
                        pltpu.VMEM((HEAD_DIM, DSA_HEADS * tq), F32),
                        pltpu.VMEM((2, groups, tk, DSA_HEADS * tq // groups), F32)],
        compiler_params=_params(("parallel", "arbitrary"), 40 << 20),
        name="dsa_attn",
    )(zb, kv, kv, bias)


def _rope_tables(positions, dim):
    inv = ROPE_THETA ** (-jnp.arange(0, dim, 2, dtype=F32) / dim)
    ang = positions.astype(F32)[..., None] * inv
    cos, sin = jnp.cos(ang), jnp.sin(ang)
    reps = LANES // dim
    cosf = jnp.tile(jnp.concatenate([cos, cos], axis=-1), (1, 1, reps))
    sinf = jnp.tile(jnp.concatenate([-sin, sin], axis=-1), (1, 1, reps))
    return cosf.reshape(-1, LANES), sinf.reshape(-1, LANES)


def kernel(x, c, positions, w_ada, b_ada, norm_g, w_in, b_forget, w_proj_fox, w_proj_dsa,
           w_gate, w_out, w_ff_gate, w_ff_up, w_ff_down):
    B, S, D = x.shape
    M = B * S
    L = w_in.shape[0]
    F = w_ff_gate.shape[2]
    assert w_in.shape[2] == D_IN and S % CHUNK == 0
    top_k = min(TOPK_MAX, S // 4)
    idx_w_scale = IDX_WIDTH ** -0.5

    tm = _tile(M, 1024)
    tn_in = _tile(DSA_WIDTH, 1024)
    tn_merge = _tile(D, 256)
    tn_out = _tile(D, 1024)
    tn_ff = _tile(F, 512)
    tm_ff = _tile(M, 2048 if tn_ff <= 256 else 1024)
    tm_down, tn_down = _tile(M, 512), _tile(D, 512)

    wb16 = lambda w: w.astype(BF16)
    w_qb32, w_qi32 = w_in[:, :, OFF_QB:OFF_KB], w_in[:, :, OFF_QI:OFF_KI]
    w_small = jnp.concatenate([wb16(w_in[:, :, OFF_KB:OFF_QI]), wb16(w_in[:, :, OFF_KI:D_IN]),
                               wb16(w_in[:, :, OFF_FA:OFF_QB]),
                               jnp.zeros((L, D, SMALL_N - 2 * HEAD_DIM - MISC_FA - FOX_HEADS), BF16)],
                              axis=2)
    w_gate16, w_pf16, w_pd16, w_out16 = wb16(w_gate), wb16(w_proj_fox), wb16(w_proj_dsa), wb16(w_out)
    w_fg16, w_fu16, w_fd16 = wb16(w_ff_gate), wb16(w_ff_up), wb16(w_ff_down)
    fbias = jnp.zeros((L, 1, LANES), F32).at[:, 0, MISC_FA:MISC_FA + FOX_HEADS].set(b_forget)

    cos_h, sin_h = _rope_tables(positions, HEAD_DIM)
    cos_i, sin_i = _rope_tables(positions, IDX_DIM)
    tab_spec = pl.BlockSpec((tm, LANES), lambda i, j: (i, 0))
    tables = [(t, tab_spec) for t in (cos_h, sin_h, cos_i, sin_i)]

    mods = _ada(c, w_ada, b_ada)

    _, h = _norm(x, None, mods[0], mods[0], norm_g[0], norm_g[0],
                 scale_row=1, shift_row=0, gb_row=0)
    for l in range(L):
        h2d = h.reshape(M, D)
        za = _matmul_ws(h2d, w_in, [], 3 * FOX_WIDTH, functools.partial(_epi_fox_qkv, tn=tn_in),
                        layer=l, col0=0, tm=tm, tn=tn_in, name="in_proj_fox")
        zqb = _matmul_ws(h2d, w_qb32, [cos_h * LOGIT_SCALE, sin_h * LOGIT_SCALE], DSA_WIDTH,
                         functools.partial(_epi_rope, half=HEAD_DIM // 2),
                         layer=l, col0=0, tm=tm, tn=tn_in, name="in_proj_qb")
        zqi = _matmul_ws(h2d, w_qi32, [cos_i, sin_i], IDX_WIDTH,
                         functools.partial(_epi_rope, half=IDX_DIM // 2),
                         layer=l, col0=0, tm=tm, tn=tn_in, name="in_proj_qi")
        kv, misc = _matmul(
            [h2d], [(0, w_small, 0)],
            tables + [(fbias[l], pl.BlockSpec((1, LANES), lambda i, j: (0, 0)))],
            [(jax.ShapeDtypeStruct((M, 2 * HEAD_DIM), BF16),
              pl.BlockSpec((tm, 2 * HEAD_DIM), lambda i, j: (i, 0))),
             (jax.ShapeDtypeStruct((M, LANES), F32), pl.BlockSpec((tm, LANES), lambda i, j: (i, 0)))],
            functools.partial(_epi_in_small, idx_w_scale=idx_w_scale), layer=l, tm=tm, tn=SMALL_N, n_tiles=1,
            name="in_proj_small")
        misc3 = misc.reshape(B, S, LANES)
        ki = misc3[:, :, MISC_KI:MISC_KI + IDX_DIM].astype(BF16)
        w_it = misc3[:, :, MISC_WI:MISC_WI + IDX_HEADS].transpose(0, 2, 1)
        log_f = misc3[:, :, MISC_FA:MISC_FA + FOX_HEADS].transpose(0, 2, 1)
        cum = _cumsum(log_f).transpose(0, 2, 1)

        y_a = _fox(za, cum, B, S)
        bias = _dsa_index(zqi, ki, w_it, B, S, top_k)
        y_b = _dsa_attn(zqb, kv, bias, B, S)

        n_gate_tiles = D // tn_merge
        (mixed,) = _matmul(
            [h2d, y_a, y_b],
            [(0, w_gate16, 0), (0, w_gate16, n_gate_tiles), (1, w_pf16, 0), (2, w_pd16, 0)],
            [], [(jax.ShapeDtypeStruct((M, D), BF16), pl.BlockSpec((tm, tn_merge), lambda i, j: (i, j)))],
            _epi_merge, layer=l, tm=tm, tn=tn_merge, n_tiles=n_gate_tiles, name="gate_merge")
        (yo,) = _matmul(
            [mixed], [(0, w_out16, 0)], [],
            [(jax.ShapeDtypeStruct((M, D), F32), pl.BlockSpec((tm, tn_out), lambda i, j: (i, j)))],
            _epi_plain, layer=l, tm=tm, tn=tn_out, n_tiles=D // tn_out, name="out_proj")
        x, h = _norm(x, yo.reshape(B, S, D), mods[l], mods[l], norm_g[l], norm_g[l],
                     gate_row=2, ga_row=1, scale_row=4, shift_row=3, gb_row=2)

        (t,) = _matmul(
            [h.reshape(M, D)], [(0, w_fg16, 0), (0, w_fu16, 0)], [],
            [(jax.ShapeDtypeStruct((M, F), BF16), pl.BlockSpec((tm_ff, tn_ff), lambda i, j: (i, j)))],
            _epi_swiglu, layer=l, tm=tm_ff, tn=tn_ff, n_tiles=F // tn_ff, name="ffn_gate_up")
        (ff,) = _matmul(
            [t], [(0, w_fd16, 0)], [],
            [(jax.ShapeDtypeStruct((M, D), F32), pl.BlockSpec((tm_down, tn_down), lambda i, j: (i, j)))],
            _epi_plain, layer=l, tm=tm_down, tn=tn_down, n_tiles=D // tn_down, name="ffn_down")
        last = l == L - 1
        nl = l if last else l + 1
        x, h = _norm(x, ff.reshape(B, S, D), mods[l], mods[nl], norm_g[l], norm_g[nl],
                     gate_row=5, ga_row=3, scale_row=1, shift_row=0, gb_row=0, emit_h=not last)
    return x
```

```python
import functools

import jax
import jax.numpy as jnp
from jax import lax
from jax.experimental import pallas as pl
from jax.experimental.pallas import tpu as pltpu

F32 = jnp.float32
BF16 = jnp.bfloat16

CHUNK = 64
HEAD_DIM = 128
FOX_HEADS = 16
DSA_HEADS = 16
FOX_WIDTH = FOX_HEADS * HEAD_DIM
DSA_WIDTH = DSA_HEADS * HEAD_DIM
IDX_HEADS = 32
IDX_DIM = 64
IDX_WIDTH = IDX_HEADS * IDX_DIM
TOPK_MAX = 256
ROPE_THETA = 10000.0
EPS = 1e-6
N_MOD = 6

OFF_QA = 0
OFF_FA = 3 * FOX_WIDTH
OFF_QB = OFF_FA + FOX_HEADS
OFF_KB = OFF_QB + DSA_WIDTH
OFF_QI = OFF_KB + 2 * HEAD_DIM
OFF_KI = OFF_QI + IDX_WIDTH
OFF_WI = OFF_KI + IDX_DIM
D_IN = OFF_WI + IDX_HEADS

BIG_QB = 3 * FOX_WIDTH
BIG_QI = BIG_QB + DSA_WIDTH
BIG_N = BIG_QI + IDX_WIDTH
MISC_KI = 0
MISC_WI = IDX_DIM
MISC_FA = IDX_DIM + IDX_HEADS
SMALL_N = 3 * HEAD_DIM

LANES = 128
V7X_VMEM_BYTES = 64 * 1024 * 1024
VMEM_BUDGET = V7X_VMEM_BYTES - 8 * 1024 * 1024

DSA_IDX_TQ = 256
DSA_ATT_TQ = 128
DSA_TK = 256

NEG = -1e30
INT_MIN = -(2 ** 31)
LOG2E = 1.4426950408889634
LOGIT_SCALE = HEAD_DIM ** -0.5 * LOG2E


def _params(semantics, vmem_bytes):
    return pltpu.CompilerParams(dimension_semantics=semantics,
                                vmem_limit_bytes=int(min(vmem_bytes, VMEM_BUDGET)))


def _tile(n, pref):
    if n <= pref:
        return n
    t = (pref // LANES) * LANES
    while t > LANES and n % t:
        t -= LANES
    assert n % t == 0, (n, pref)
    return t


def _ada_kernel(c_ref, w_ref, b_ref, o_ref):
    c = c_ref[...]
    s = c * jax.nn.sigmoid(c)
    acc = jnp.dot(s.astype(BF16), w_ref[...].astype(BF16), preferred_element_type=F32)
    o_ref[...] = acc[None] + b_ref[...]


def _ada(c, w_ada, b_ada):
    B, D = c.shape
    L = b_ada.shape[0]
    N = w_ada.shape[1]
    rows = 8
    cp = jnp.zeros((rows, D), F32).at[:B].set(c)
    tn = _tile(N, 512)
    out = pl.pallas_call(
        _ada_kernel,
        grid=(N // tn,),
        in_specs=[pl.BlockSpec((rows, D), lambda j: (0, 0)),
                  pl.BlockSpec((D, tn), lambda j: (0, j)),
                  pl.BlockSpec((L, 1, tn), lambda j: (0, 0, j))],
        out_specs=pl.BlockSpec((L, rows, tn), lambda j: (0, 0, j)),
        out_shape=jax.ShapeDtypeStruct((L, rows, N), F32),
        compiler_params=_params(("parallel",), 4 * D * tn * 4 + (8 << 20)),
        name="ada",
    )(cp, w_ada, b_ada.reshape(L, 1, N))
    return out[:, :B].reshape(L, B, N_MOD, D)


def _rms(v, g):
    return v * lax.rsqrt(jnp.mean(v * v, axis=-1, keepdims=True) + EPS) * g


def _norm_kernel(*refs, resid, emit_h, gate_row, ga_row, scale_row, shift_row, gb_row):
    refs = list(refs)
    x_ref = refs.pop(0)
    y_ref = refs.pop(0) if resid else None
    moda_ref, modb_ref, ga_ref, gb_ref = refs[:4]
    outs = refs[4:]
    x = x_ref[0]
    if resid:
        gate = moda_ref[0, gate_row:gate_row + 1, :]
        x = x + gate * _rms(y_ref[0], ga_ref[ga_row:ga_row + 1, :])
        outs.pop(0)[0] = x
    if emit_h:
        scale = modb_ref[0, scale_row:scale_row + 1, :]
        shift = modb_ref[0, shift_row:shift_row + 1, :]
        h = _rms(x, gb_ref[gb_row:gb_row + 1, :]) * (1.0 + scale) + shift
        outs.pop(0)[0] = h.astype(BF16)


def _norm(x, y, moda, modb, ga, gb, *, gate_row=0, ga_row=0, scale_row=0, shift_row=0,
          gb_row=0, emit_h=True):
    B, S, D = x.shape
    resid = y is not None
    ts = _tile(S, 256)
    row = pl.BlockSpec((1, ts, D), lambda b, i: (b, i, 0))
    mod = pl.BlockSpec((1, N_MOD, D), lambda b, i: (b, 0, 0))
    gsp = pl.BlockSpec((4, D), lambda b, i: (0, 0))
    in_specs = [row] + ([row] if resid else []) + [mod, mod, gsp, gsp]
    args = [x] + ([y] if resid else []) + [moda, modb, ga, gb]
    out_shape, out_specs = [], []
    if resid:
        out_shape.append(jax.ShapeDtypeStruct((B, S, D), F32))
        out_specs.append(row)
    if emit_h:
        out_shape.append(jax.ShapeDtypeStruct((B, S, D), BF16))
        out_specs.append(row)
    n_rows_f32 = 1 + 2 * resid + 0.5 * emit_h
    outs = pl.pallas_call(
        functools.partial(_norm_kernel, resid=resid, emit_h=emit_h, gate_row=gate_row,
                          ga_row=ga_row, scale_row=scale_row, shift_row=shift_row,
                          gb_row=gb_row),
        grid=(B, S // ts),
        in_specs=in_specs, out_specs=out_specs, out_shape=out_shape,
        compiler_params=_params(("parallel", "parallel"),
                                2 * n_rows_f32 * ts * D * 4 + 6 * ts * D * 4 + (4 << 20)),
        name="norm",
    )(*args)
    outs = list(outs)
    x_new = outs.pop(0) if resid else None
    h = outs.pop(0) if emit_h else None
    return x_new, h


def _mm_kernel(*refs, n_a, pairs, n_extra, epilogue):
    n_w = len(pairs)
    a_refs = refs[:n_a]
    w_refs = refs[n_a:n_a + n_w]
    extra = refs[n_a + n_w:n_a + n_w + n_extra]
    outs = refs[n_a + n_w + n_extra:]
    accs = [jnp.dot(a_refs[ai][...], w_refs[wi][...], preferred_element_type=F32)
            for wi, ai in enumerate(pairs)]
    epilogue(accs, extra, outs)


def _matmul(a_list, w_list, extras, outs, epilogue, *, layer, tm, tn, n_tiles, name):
    M = a_list[0].shape[0]
    in_specs = [pl.BlockSpec((tm, a.shape[1]), lambda i, j: (i, 0)) for a in a_list]
    vmem = sum(2 * tm * a.shape[1] * a.dtype.itemsize for a in a_list)
    for _, w, off in w_list:
        in_specs.append(pl.BlockSpec((None, w.shape[1], tn),
                                     lambda i, j, off=off: (layer, 0, j + off)))
        vmem += 2 * w.shape[1] * tn * w.dtype.itemsize + 2 * tm * tn * 4
    in_specs += [sp for _, sp in extras]
    vmem += sum(2 * tm * LANES * 4 for _ in extras)
    vmem += sum(2 * tm * sp.block_shape[-1] * sd.dtype.itemsize for sd, sp in outs)
    return pl.pallas_call(
        functools.partial(_mm_kernel, n_a=len(a_list), pairs=tuple(ai for ai, _, _ in w_list),
                          n_extra=len(extras), epilogue=epilogue),
        grid=(M // tm, n_tiles),
        in_specs=in_specs,
        out_specs=[sp for _, sp in outs],
        out_shape=[sd for sd, _ in outs],
        compiler_params=_params(("parallel", "arbitrary"), vmem + (4 << 20)),
        name=name,
    )(*a_list, *[w for _, w, _ in w_list], *[e for e, _ in extras])


def _rope_slab(a, cosf, sinf, half):
    if 2 * half == LANES:
        partner = pltpu.roll(a, half, 1)
    else:
        lane = lax.broadcasted_iota(jnp.int32, a.shape, 1)
        partner = jnp.where(lane % (2 * half) < half,
                            pltpu.roll(a, LANES - half, 1), pltpu.roll(a, half, 1))
    return a * cosf + partner * sinf


def _epi_in_big(accs, extra, outs, *, tn):
    cos_h, sin_h, cos_i, sin_i = extra
    (o_ref,) = outs
    acc = accs[0]
    j = pl.program_id(1)
    t_ka, t_qb, t_qi = FOX_WIDTH // tn, BIG_QB // tn, BIG_QI // tn

    @pl.when(j < t_ka)
    def _():
        o_ref[...] = (acc * LOGIT_SCALE).astype(BF16)

    @pl.when((j >= t_ka) & (j < t_qb))
    def _():
        o_ref[...] = acc.astype(BF16)

    def roped(cos_ref, sin_ref, half):
        cosf, sinf = cos_ref[...], sin_ref[...]
        for s in range(tn // LANES):
            sl = slice(s * LANES, (s + 1) * LANES)
            o_ref[:, sl] = _rope_slab(acc[:, sl], cosf, sinf, half).astype(BF16)

    @pl.when((j >= t_qb) & (j < t_qi))
    def _():
        roped(cos_h, sin_h, HEAD_DIM // 2)

    @pl.when(j >= t_qi)
    def _():
        roped(cos_i, sin_i, IDX_DIM // 2)


def _epi_in_small(accs, extra, outs, *, idx_w_scale):
    cos_h, sin_h, cos_i, sin_i, fbias = extra
    kv_ref, misc_ref = outs
    acc = accs[0]
    kb = _rope_slab(acc[:, 0:LANES], cos_h[...], sin_h[...], HEAD_DIM // 2)
    kv_ref[:, 0:LANES] = kb.astype(BF16)
    kv_ref[:, LANES:2 * LANES] = acc[:, LANES:2 * LANES].astype(BF16)
    a = acc[:, 2 * LANES:3 * LANES]
    ki = _rope_slab(a, cos_i[...], sin_i[...], IDX_DIM // 2)
    f = a + fbias[...]
    log_f = jnp.minimum(f, 0.0) - jnp.log1p(jnp.exp(-jnp.abs(f)))
    lane = lax.broadcasted_iota(jnp.int32, a.shape, 1)
    misc_ref[...] = jnp.where(lane < MISC_WI, ki,
                              jnp.where(lane < MISC_FA, a * idx_w_scale, log_f))


def _epi_merge(accs, extra, outs):
    ga, gb, pa, pb = accs
    outs[0][...] = (jax.nn.sigmoid(ga) * pa + jax.nn.sigmoid(gb) * pb).astype(BF16)


def _epi_plain(accs, extra, outs):
    outs[0][...] = accs[0].astype(outs[0].dtype)


def _epi_swiglu(accs, extra, outs):
    g, u = accs
    outs[0][...] = (g * jax.nn.sigmoid(g) * u).astype(BF16)


def _cumsum_kernel(x_ref, o_ref, *, width):
    S = x_ref.shape[2]
    r = lax.broadcasted_iota(jnp.int32, (width, width), 0)
    c = lax.broadcasted_iota(jnp.int32, (width, width), 1)
    upper = jnp.where(r <= c, 1.0, 0.0).astype(BF16)
    carry = jnp.zeros((x_ref.shape[1], 1), F32)
    for i in range(S // width):
        x = x_ref[0, :, i * width:(i + 1) * width]
        hi = x.astype(BF16)
        r1 = x - hi.astype(F32)
        mid = r1.astype(BF16)
        lo = (r1 - mid.astype(F32)).astype(BF16)
        y = (jnp.dot(hi, upper, preferred_element_type=F32)
             + jnp.dot(mid, upper, preferred_element_type=F32)
             + jnp.dot(lo, upper, preferred_element_type=F32)) + carry
        o_ref[0, :, i * width:(i + 1) * width] = y
        carry = y[:, width - 1:width]


def _cumsum(x):
    B, H, S = x.shape
    width = _tile(S, 256)
    spec = pl.BlockSpec((1, H, S), lambda b: (b, 0, 0))
    return pl.pallas_call(
        functools.partial(_cumsum_kernel, width=width),
        grid=(B,), in_specs=[spec], out_specs=spec,
        out_shape=jax.ShapeDtypeStruct((B, H, S), F32),
        compiler_params=_params(("parallel",), 16 << 20),
        name="cumsum",
    )(x)


_NT = (((1,), (1,)), ((), ()))
_TN = (((0,), (0,)), ((), ()))


def _fox_kernel(q_ref, k_ref, v_ref, cum_ref, o_ref, ck_ref, m_ref, l_ref, acc_ref, s0_ref,
                qt_ref, vt_ref, *, tq, hg):
    g = pl.program_id(1)
    i = pl.program_id(2)
    S = k_ref.shape[0]
    D = HEAD_DIM

    @pl.when(i == 0)
    def _():
        lane = lax.broadcasted_iota(jnp.int32, (1, FOX_HEADS), 1)
        for c in range(S // tq):
            blk = cum_ref[0, c * tq:(c + 1) * tq, :]
            for j in range(hg):
                onehot = jnp.where(lane == g * hg + j, LOG2E, 0.0)
                col = jnp.sum(blk * onehot, axis=1, keepdims=True)
                ck_ref[j, c * tq:(c + 1) * tq, :] = jnp.broadcast_to(col, (tq, LANES))
                vt_ref[j, :, c * tq:(c + 1) * tq] = v_ref[
                    c * tq:(c + 1) * tq, j * D:(j + 1) * D].astype(F32).T.astype(BF16)

    for j in range(hg):
        qt_ref[j] = q_ref[:, j * D:(j + 1) * D].astype(F32).T.astype(BF16)
    m_ref[...] = jnp.full(m_ref.shape, NEG, F32)
    l_ref[...] = jnp.zeros(l_ref.shape, F32)
    acc_ref[...] = jnp.zeros(acc_ref.shape, F32)

    def logits(c, j):
        start = pl.multiple_of(c * tq, tq)
        return jnp.dot(k_ref[pl.ds(start, tq), j * D:(j + 1) * D], qt_ref[j],
                       preferred_element_type=F32)

    s0_ref[...] = logits(0, 0)

    def chunk(c, diagonal):
        start = pl.multiple_of(c * tq, tq)
        s_next = s0_ref[...]
        for j in range(hg):
            s = s_next
            if j + 1 < hg:
                s_next = logits(c, j + 1)
            elif not diagonal:
                s0_ref[...] = logits(c + 1, 0)
            s = s - jnp.tile(ck_ref[j, pl.ds(start, tq), :], (1, tq // LANES))
            if diagonal:
                key = lax.broadcasted_iota(jnp.int32, s.shape, 0)
                qry = lax.broadcasted_iota(jnp.int32, s.shape, 1)
                s = jnp.where(key <= qry, s, NEG)
            m_prev = m_ref[j]
            m_new = jnp.maximum(m_prev, jnp.max(s, axis=0, keepdims=True))
            alpha = jnp.exp2(m_prev - m_new)
            p = jnp.exp2(s - m_new)
            l_ref[j] = alpha * l_ref[j] + jnp.sum(p, axis=0, keepdims=True)
            acc_ref[j] = alpha * acc_ref[j] + jnp.dot(
                vt_ref[j, :, pl.ds(start, tq)], p.astype(BF16),
                preferred_element_type=F32)
            m_ref[j] = m_new

    def body(c, carry):
        chunk(c, False)
        return carry

    lax.fori_loop(0, i, body, 0)
    chunk(i, True)
    for j in range(hg):
        o_ref[:, j * D:(j + 1) * D] = (acc_ref[j] / l_ref[j]).T.astype(BF16)


def _fox(zb, cum, B, S):
    M = B * S
    tq = _tile(S, 512)
    nq = S // tq
    H = FOX_HEADS
    hg = 4
    G = H // hg
    W = hg * HEAD_DIM
    return pl.pallas_call(
        functools.partial(_fox_kernel, tq=tq, hg=hg),
        grid=(B, G, nq),
        in_specs=[pl.BlockSpec((tq, W), lambda b, g, i: (b * nq + i, g)),
                  pl.BlockSpec((S, W), lambda b, g, i: (b, G + g)),
                  pl.BlockSpec((S, W), lambda b, g, i: (b, 2 * G + g)),
                  pl.BlockSpec((1, S, H), lambda b, g, i: (b, 0, 0))],
        out_specs=pl.BlockSpec((tq, W), lambda b, g, i: (b * nq + i, g)),
        out_shape=jax.ShapeDtypeStruct((M, FOX_WIDTH), BF16),
        scratch_shapes=[pltpu.VMEM((hg, S, LANES), F32),
                        pltpu.VMEM((hg, 1, tq), F32), pltpu.VMEM((hg, 1, tq), F32),
                        pltpu.VMEM((hg, HEAD_DIM, tq), F32),
                        pltpu.VMEM((tq, tq), F32),
                        pltpu.VMEM((hg, HEAD_DIM, tq), BF16),
                        pltpu.VMEM((hg, HEAD_DIM, S), BF16)],
        compiler_params=_params(("parallel", "parallel", "arbitrary"), 40 << 20),
        name="fox_attn",
    )(zb, zb, zb, cum)


def _dsa_index_kernel(qi_ref, ki_ref, wt_ref, bias_ref, qt_ref, key_ref, x_ref,
                      *, tq, tc, top_k, n_idx_bits):
    i = pl.program_id(1)
    S = ki_ref.shape[1]
    n_chunks = ((i + 1) * tq + tc - 1) // tc

    for p in range(IDX_WIDTH // LANES):
        blk = qi_ref[:, p * LANES:(p + 1) * LANES].astype(F32).T
        qt_ref[:, (2 * p) * tq:(2 * p + 1) * tq] = blk[:IDX_DIM].astype(BF16)
        qt_ref[:, (2 * p + 1) * tq:(2 * p + 2) * tq] = blk[IDX_DIM:].astype(BF16)
    bias_ref[...] = jnp.full(bias_ref.shape, NEG, BF16)

    t = i * tq + lax.broadcasted_iota(jnp.int32, (1, tq), 1)
    lim = (t // CHUNK + 1) * CHUNK

    def row_of(c):
        return c * tc + lax.broadcasted_iota(jnp.int32, (tc, tq), 0)

    def score_chunk(c, carry):
        start = pl.multiple_of(c * tc, tc)
        rel = jnp.dot(ki_ref[0, pl.ds(start, tc), :], qt_ref[...],
                      preferred_element_type=F32)
        acc = jnp.zeros((tc, tq), F32)
        for h in range(IDX_HEADS):
            acc = acc + wt_ref[0, h:h + 1, :] * jnp.maximum(rel[:, h * tq:(h + 1) * tq], 0.0)
        bits = pltpu.bitcast(acc, jnp.int32)
        key = bits ^ ((bits >> 31) & 0x7FFFFFFF)
        key = jnp.where(acc == 0.0, 0, key)
        key_ref[pl.ds(start, tc), :] = jnp.where(row_of(c) < lim, key, INT_MIN)
        return carry

    lax.fori_loop(0, n_chunks, score_chunk, 0)

    pair = 2 if (S // tc) % 2 == 0 else 1
    n_cnt = (n_chunks + pair - 1) // pair
    tcc = pair * tc
    if pair == 2:
        @pl.when(n_chunks % 2 == 1)
        def _():
            key_ref[pl.ds(pl.multiple_of(n_chunks * tc, tc), tc), :] = jnp.full(
                (tc, tq), INT_MIN, jnp.int32)

    def count(*preds):
        def body(c, cnts):
            kc = key_ref[pl.ds(pl.multiple_of(c * tcc, tcc), tcc), :]
            return tuple(cnt + jnp.sum(jnp.where(pred(kc, c), 1, 0).reshape(tcc // 32, 32, tq), axis=0)
                         for cnt, pred in zip(cnts, preds))
        cnts = lax.fori_loop(0, n_cnt, body,
                             tuple(jnp.zeros((32, tq), jnp.int32) for _ in preds))
        return [jnp.sum(cnt, axis=0, keepdims=True) for cnt in cnts]

    def cnt_row(c):
        return c * tcc + lax.broadcasted_iota(jnp.int32, (tcc, tq), 0)

    def bit_body(it, T):
        cand = T + lax.shift_left(jnp.int32(1), 31 - it)
        (n_ge,) = count(lambda kc, c: kc >= cand)
        return jnp.where(n_ge >= top_k, cand, T)

    T = lax.fori_loop(0, 32, bit_body, jnp.full((1, tq), INT_MIN, jnp.int32))
    n_ge, n_gt = count(lambda kc, c: kc >= T, lambda kc, c: kc > T)
    need = top_k - n_gt
    excess = jnp.where((n_ge > top_k) & (T != INT_MIN), 1, 0)
    x_ref[...] = jnp.full((1, tq), S, jnp.int32)

    @pl.when(jnp.max(excess) > 0)
    def _():
        def x_body(it, X):
            cand = X + lax.shift_left(jnp.int32(1), n_idx_bits - 1 - it)
            (n,) = count(lambda kc, c: (kc == T) & (cnt_row(c) < cand))
            return jnp.where(n <= need - 1, cand, X)
        x_ref[...] = lax.fori_loop(0, n_idx_bits, x_body, jnp.zeros((1, tq), jnp.int32))

    X = x_ref[...]

    def write_chunk(c, carry):
        start = pl.multiple_of(c * tc, tc)
        kc = key_ref[pl.ds(start, tc), :]
        row = row_of(c)
        tie = jnp.where(kc == T, jnp.where(row <= X, 0.0, NEG), NEG)
        sel = jnp.where(kc > T, 0.0, tie)
        bias_ref[0, 0, pl.ds(start, tc), :] = jnp.where(row < lim, sel, NEG).astype(BF16)
        return carry

    lax.fori_loop(0, n_chunks, write_chunk, 0)


def _dsa_index(zb, ki, wt, B, S, top_k):
    tq, tc = _tile(S, DSA_IDX_TQ), _tile(S, DSA_TK)
    nq = S // tq
    n_idx_bits = max(1, (S - 1).bit_length())
    return pl.pallas_call(
        functools.partial(_dsa_index_kernel, tq=tq, tc=tc, top_k=top_k, n_idx_bits=n_idx_bits),
        grid=(B, nq),
        in_specs=[pl.BlockSpec((tq, IDX_WIDTH), lambda b, i: (b * nq + i, BIG_QI // IDX_WIDTH)),
                  pl.BlockSpec((1, S, IDX_DIM), lambda b, i: (b, 0, 0)),
                  pl.BlockSpec((1, IDX_HEADS, tq), lambda b, i: (b, 0, i))],
        out_specs=pl.BlockSpec((1, 1, S, tq), lambda b, i: (b, i, 0, 0)),
        out_shape=jax.ShapeDtypeStruct((B, nq, S, tq), BF16),
        scratch_shapes=[pltpu.VMEM((IDX_DIM, IDX_HEADS * tq), BF16),
                        pltpu.VMEM((S, tq), jnp.int32),
                        pltpu.VMEM((1, tq), jnp.int32)],
        compiler_params=_params(("parallel", "arbitrary"), 40 << 20),
        name="dsa_index",
    )(zb, ki, wt)


def _dsa_attn_kernel(q_ref, k_ref, v_ref, bias_ref, o_ref, qt_ref, m_ref, l_ref, acc_ref, s_ref,
                     *, tq, tk, groups):
    i = pl.program_id(1)
    H = DSA_HEADS
    n_chunks = ((i + 1) * tq + tk - 1) // tk
    for h in range(H):
        qh = q_ref[:, h * HEAD_DIM:(h + 1) * HEAD_DIM].astype(F32)
        qt_ref[:, h * tq:(h + 1) * tq] = qh.T.astype(BF16)
    m_ref[...] = jnp.full(m_ref.shape, NEG, F32)
    l_ref[...] = jnp.zeros(l_ref.shape, F32)
    acc_ref[...] = jnp.zeros(acc_ref.shape, F32)

    gw = H * tq // groups

    n_total = k_ref.shape[0] // tk

    def stage(buf, c):
        start = pl.multiple_of(jnp.minimum(c, n_total - 1) * tk, tk)
        k = k_ref[pl.ds(start, tk), :]
        for g in range(groups):
            s_ref[buf, g] = jnp.dot(k, qt_ref[:, g * gw:(g + 1) * gw], preferred_element_type=F32)

    def consume(buf, c):
        start = pl.multiple_of(c * tk, tk)
        v = v_ref[pl.ds(start, tk), :]
        bias = bias_ref[0, 0, pl.ds(start, tk), :].astype(F32)
        bias = jnp.tile(bias, (1, gw // tq))
        for g in range(groups):
            sl = slice(g * gw, (g + 1) * gw)
            s = s_ref[buf, g] + bias
            m_prev = m_ref[:, sl]
            m_new = jnp.maximum(m_prev, jnp.max(s, axis=0, keepdims=True))
            alpha = jnp.exp2(m_prev - m_new)
            p = jnp.exp2(s - m_new)
            l_ref[:, sl] = alpha * l_ref[:, sl] + jnp.sum(p, axis=0, keepdims=True)
            acc_ref[:, sl] = alpha * acc_ref[:, sl] + lax.dot_general(
                v, p.astype(BF16), _TN, preferred_element_type=F32)
            m_ref[:, sl] = m_new

    if n_total % 2 == 0:
        stage(0, 0)

        def body(it, carry):
            stage(1, 2 * it + 1)
            consume(0, 2 * it)
            stage(0, 2 * it + 2)
            consume(1, 2 * it + 1)
            return carry

        lax.fori_loop(0, (n_chunks + 1) // 2, body, 0)
    else:
        def body(c, carry):
            stage(0, c)
            consume(0, c)
            return carry

        lax.fori_loop(0, n_chunks, body, 0)
    out = acc_ref[...] / l_ref[...]
    for h in range(H):
        o_ref[:, h * HEAD_DIM:(h + 1) * HEAD_DIM] = out[:, h * tq:(h + 1) * tq].T.astype(BF16)


def _dsa_attn(zb, kv, bias, B, S):
    M = B * S
    tq, tk = _tile(S, DSA_ATT_TQ), _tile(S, DSA_TK)
    nq = S // tq
    per_idx = bias.shape[3] // tq
    groups = 4
    return pl.pallas_call(
        functools.partial(_dsa_attn_kernel, tq=tq, tk=tk, groups=groups),
        grid=(B, nq),
        in_specs=[pl.BlockSpec((tq, DSA_WIDTH), lambda b, i: (b * nq + i, BIG_QB // DSA_WIDTH)),
                  pl.BlockSpec((S, HEAD_DIM), lambda b, i: (b, 0)),
                  pl.BlockSpec((S, HEAD_DIM), lambda b, i: (b, 1)),
                  pl.BlockSpec((1, 1, S, tq), lambda b, i: (b, i // per_idx, 0, i % per_idx))],
        out_specs=pl.BlockSpec((tq, DSA_WIDTH), lambda b, i: (b * nq + i, 0)),
        out_shape=jax.ShapeDtypeStruct((M, DSA_WIDTH), BF16),
        scratch_shapes=[pltpu.VMEM((HEAD_DIM, DSA_HEADS * tq), BF16),
                        pltpu.VMEM((1, DSA_HEADS * tq), F32),
                        pltpu.VMEM((1, DSA_HEADS * tq), F32),
                        pltpu.VMEM((HEAD_DIM, DSA_HEADS * tq), F32),
                        pltpu.VMEM((2, groups, tk, DSA_HEADS * tq // groups), F32)],
        compiler_params=_params(("parallel", "arbitrary"), 40 << 20),
        name="dsa_attn",
    )(zb, kv, kv, bias)


def _rope_tables(positions, dim):
    inv = ROPE_THETA ** (-jnp.arange(0, dim, 2, dtype=F32) / dim)
    ang = positions.astype(F32)[..., None] * inv
    cos, sin = jnp.cos(ang), jnp.sin(ang)
    reps = LANES // dim
    cosf = jnp.tile(jnp.concatenate([cos, cos], axis=-1), (1, 1, reps))
    sinf = jnp.tile(jnp.concatenate([-sin, sin], axis=-1), (1, 1, reps))
    return cosf.reshape(-1, LANES), sinf.reshape(-1, LANES)


def kernel(x, c, positions, w_ada, b_ada, norm_g, w_in, b_forget, w_proj_fox, w_proj_dsa,
           w_gate, w_out, w_ff_gate, w_ff_up, w_ff_down):
    B, S, D = x.shape
    M = B * S
    L = w_in.shape[0]
    F = w_ff_gate.shape[2]
    assert w_in.shape[2] == D_IN and S % CHUNK == 0
    top_k = min(TOPK_MAX, S // 4)
    idx_w_scale = IDX_WIDTH ** -0.5

    tm = _tile(M, 1024)
    tn_in = _tile(DSA_WIDTH, 1024)
    tn_merge = _tile(D, 256)
    tn_out = _tile(D, 1024)
    tn_ff = _tile(F, 512)
    tm_ff = _tile(M, 2048 if tn_ff <= 256 else 1024)
    tm_down, tn_down = _tile(M, 512), _tile(D, 512)

    wb16 = lambda w: w.astype(BF16)
    w_big = jnp.concatenate([wb16(w_in[:, :, OFF_QA:OFF_FA]), wb16(w_in[:, :, OFF_QB:OFF_KB]),
                             wb16(w_in[:, :, OFF_QI:OFF_KI])], axis=2)
    w_small = jnp.concatenate([wb16(w_in[:, :, OFF_KB:OFF_QI]), wb16(w_in[:, :, OFF_KI:D_IN]),
                               wb16(w_in[:, :, OFF_FA:OFF_QB]),
                               jnp.zeros((L, D, SMALL_N - 2 * HEAD_DIM - MISC_FA - FOX_HEADS), BF16)],
                              axis=2)
    w_gate16, w_pf16, w_pd16, w_out16 = wb16(w_gate), wb16(w_proj_fox), wb16(w_proj_dsa), wb16(w_out)
    w_fg16, w_fu16, w_fd16 = wb16(w_ff_gate), wb16(w_ff_up), wb16(w_ff_down)
    fbias = jnp.zeros((L, 1, LANES), F32).at[:, 0, MISC_FA:MISC_FA + FOX_HEADS].set(b_forget)

    cos_h, sin_h = _rope_tables(positions, HEAD_DIM)
    cos_i, sin_i = _rope_tables(positions, IDX_DIM)
    tab_spec = pl.BlockSpec((tm, LANES), lambda i, j: (i, 0))
    tables = [(t, tab_spec) for t in (cos_h, sin_h, cos_i, sin_i)]
    q_tables = [(t, tab_spec) for t in (cos_h * LOGIT_SCALE, sin_h * LOGIT_SCALE, cos_i, sin_i)]

    mods = _ada(c, w_ada, b_ada)

    _, h = _norm(x, None, mods[0], mods[0], norm_g[0], norm_g[0],
                 scale_row=1, shift_row=0, gb_row=0)
    for l in range(L):
        h2d = h.reshape(M, D)
        (zb,) = _matmul(
            [h2d], [(0, w_big, 0)], q_tables,
            [(jax.ShapeDtypeStruct((M, BIG_N), BF16), pl.BlockSpec((tm, tn_in), lambda i, j: (i, j)))],
            functools.partial(_epi_in_big, tn=tn_in), layer=l, tm=tm, tn=tn_in, n_tiles=BIG_N // tn_in,
            name="in_proj")
        kv, misc = _matmul(
            [h2d], [(0, w_small, 0)],
            tables + [(fbias[l], pl.BlockSpec((1, LANES), lambda i, j: (0, 0)))],
            [(jax.ShapeDtypeStruct((M, 2 * HEAD_DIM), BF16),
              pl.BlockSpec((tm, 2 * HEAD_DIM), lambda i, j: (i, 0))),
             (jax.ShapeDtypeStruct((M, LANES), F32), pl.BlockSpec((tm, LANES), lambda i, j: (i, 0)))],
            functools.partial(_epi_in_small, idx_w_scale=idx_w_scale), layer=l, tm=tm, tn=SMALL_N, n_tiles=1,
            name="in_proj_small")
        misc3 = misc.reshape(B, S, LANES)
        ki = misc3[:, :, MISC_KI:MISC_KI + IDX_DIM].astype(BF16)
        w_it = misc3[:, :, MISC_WI:MISC_WI + IDX_HEADS].transpose(0, 2, 1)
        log_f = misc3[:, :, MISC_FA:MISC_FA + FOX_HEADS].transpose(0, 2, 1)
        cum = _cumsum(log_f).transpose(0, 2, 1)

        y_a = _fox(zb, cum, B, S)
        bias = _dsa_index(zb, ki, w_it, B, S, top_k)
        y_b = _dsa_attn(zb, kv, bias, B, S)

        n_gate_tiles = D // tn_merge
        (mixed,) = _matmul(
            [h2d, y_a, y_b],
            [(0, w_gate16, 0), (0, w_gate16, n_gate_tiles), (1, w_pf16, 0), (2, w_pd16, 0)],
            [], [(jax.ShapeDtypeStruct((M, D), BF16), pl.BlockSpec((tm, tn_merge), lambda i, j: (i, j)))],
            _epi_merge, layer=l, tm=tm, tn=tn_merge, n_tiles=n_gate_tiles, name="gate_merge")
        (yo,) = _matmul(
            [mixed], [(0, w_out16, 0)], [],
            [(jax.ShapeDtypeStruct((M, D), F32), pl.BlockSpec((tm, tn_out), lambda i, j: (i, j)))],
            _epi_plain, layer=l, tm=tm, tn=tn_out, n_tiles=D // tn_out, name="out_proj")
        x, h = _norm(x, yo.reshape(B, S, D), mods[l], mods[l], norm_g[l], norm_g[l],
                     gate_row=2, ga_row=1, scale_row=4, shift_row=3, gb_row=2)

        (t,) = _matmul(
            [h.reshape(M, D)], [(0, w_fg16, 0), (0, w_fu16, 0)], [],
            [(jax.ShapeDtypeStruct((M, F), BF16), pl.BlockSpec((tm_ff, tn_ff), lambda i, j: (i, j)))],
            _epi_swiglu, layer=l, tm=tm_ff, tn=tn_ff, n_tiles=F // tn_ff, name="ffn_gate_up")
        (ff,) = _matmul(
            [t], [(0, w_fd16, 0)], [],
            [(jax.ShapeDtypeStruct((M, D), F32), pl.BlockSpec((tm_down, tn_down), lambda i, j: (i, j)))],
            _epi_plain, layer=l, tm=tm_down, tn=tn_down, n_tiles=D // tn_down, name="ffn_down")
        last = l == L - 1
        nl = l if last else l + 1
        x, h = _norm(x, ff.reshape(B, S, D), mods[l], mods[nl], norm_g[l], norm_g[nl],
                     gate_row=5, ga_row=3, scale_row=1, shift_row=0, gb_row=0, emit_h=not last)
    return x
```
